```python
import math
import jax
import jax.numpy as jnp
from jax import lax
import numpy as np

D_MODEL = 1024
BATCH = 4
SEQ = 4096
DEPTH = 4
DEC_BATCH = 128
DEC_SEQ = 1
PAST_LEN = 8192
PAGE_SIZE = 128

N_MIXERS = 4
N_HEADS = 16
HEAD_DIM = 64
N_KV = 4
GROUP = N_HEADS // N_KV
Q_W = N_HEADS * HEAD_DIM
KV_W = N_KV * HEAD_DIM
QKV_W = Q_W + 2 * KV_W
Q_BLOCK = 128
MLA_Q_RANK = 768
MLA_KV_RANK = 256
MLA_NOPE = 64
MLA_ROPE = 32
MLA_V = 64
MLA_IN_W = MLA_Q_RANK + MLA_KV_RANK + MLA_ROPE
WINDOW = 128
N_MEM = 256
MEM_HEADS = 4
MEM_HEAD_DIM = 128
MEM_W = MEM_HEADS * MEM_HEAD_DIM
D_FF = 2816
CONV_W = 3
ROPE_THETA = 10000.0
EPS = 1e-6
FORGET_BIAS = 3.0

N_PAGES = PAST_LEN // PAGE_SIZE
N_POOL = (DEC_BATCH * N_PAGES * 5) // 4
WIN_BUF = min(WINDOW, PAST_LEN)
N_OF = tuple(len(range(m, DEPTH, N_MIXERS)) for m in range(N_MIXERS))

kernel_name = 'hybrid_sb_mla_fox_swa_decoder_step'


def rmsnorm(x, g):
    xf = x.astype(jnp.float32)
    y = xf * lax.rsqrt(jnp.mean(xf * xf, axis=-1, keepdims=True) + EPS)
    return (y * g.astype(jnp.float32)).astype(x.dtype)


def rope(x, pos):
    half = x.shape[-1] // 2
    inv = jnp.exp(-math.log(ROPE_THETA) * jnp.arange(half, dtype=jnp.float32) / half)
    ang = pos.astype(jnp.float32)[:, None] * inv[None, :]
    cos = jnp.cos(ang)[:, None, :]
    sin = jnp.sin(ang)[:, None, :]
    xf = x.astype(jnp.float32)
    x1, x2 = xf[..., :half], xf[..., half:]
    return jnp.concatenate([x1 * cos - x2 * sin, x2 * cos + x1 * sin], axis=-1).astype(x.dtype)


def split_qkv(a, B, T):
    q = a[..., :Q_W].reshape(B, T, N_HEADS, HEAD_DIM)
    k = a[..., Q_W:Q_W + KV_W].reshape(B, T, N_KV, HEAD_DIM)
    v = a[..., Q_W + KV_W:QKV_W].reshape(B, T, N_KV, HEAD_DIM)
    return q, k, v


def gather_pages(pool, page_table):
    g = pool[page_table]
    return g.reshape(g.shape[0], g.shape[1] * g.shape[2], *g.shape[3:])


def over_query_blocks(fn, q_args, qpos):
    T = qpos.shape[0]
    if T <= Q_BLOCK or T % Q_BLOCK:
        return fn(*q_args, qpos)
    nb = T // Q_BLOCK
    def to_blocks(a):
        return jnp.swapaxes(a.reshape(a.shape[0], nb, Q_BLOCK, *a.shape[2:]), 0, 1)
    xs = tuple(to_blocks(a) for a in q_args) + (qpos.reshape(nb, Q_BLOCK),)
    out = lax.map(lambda blk: fn(*blk), xs)
    out = jnp.swapaxes(out, 0, 1)
    return out.reshape(out.shape[0], T, *out.shape[3:])


def stick_breaking_attend(q, qpos, k, v, kpos):
    B, Tq = q.shape[:2]
    qg = q.reshape(B, Tq, N_KV, GROUP, HEAD_DIM)
    z = jnp.einsum('bqkgd,bskd->bkgqs', qg, k, preferred_element_type=jnp.float32) * HEAD_DIM ** -0.5
    mask = kpos[None, :] < qpos[:, None]
    log_beta = jax.nn.log_sigmoid(z)
    log_rest = jnp.where(mask, jax.nn.log_sigmoid(-z), 0.0)
    tail = lax.cumsum(log_rest, axis=4, reverse=True) - log_rest
    w = jnp.where(mask, jnp.exp(log_beta + tail), 0.0)
    o = jnp.einsum('bkgqs,bskd->bqkgd', w.astype(v.dtype), v)
    return o.reshape(B, Tq, Q_W)


def mla_attend(q_lat, q_rope, qpos, ckv, kr, kpos, w_uv):
    s = (jnp.einsum('bqhc,bsc->bhqs', q_lat, ckv, preferred_element_type=jnp.float32)
         + jnp.einsum('bqhr,bsr->bhqs', q_rope, kr, preferred_element_type=jnp.float32))
    s = s * (MLA_NOPE + MLA_ROPE) ** -0.5
    s = jnp.where(kpos[None, :] <= qpos[:, None], s, -jnp.inf)
    p = jax.nn.softmax(s, axis=-1)
    o_lat = jnp.einsum('bhqs,bsc->bqhc', p.astype(ckv.dtype), ckv)
    return jnp.einsum('bqhc,chv->bqhv', o_lat, w_uv)


def fox_attend(q, cq, qpos, k, v, ck, kpos):
    B, Tq = q.shape[:2]
    qg = q.reshape(B, Tq, N_KV, GROUP, HEAD_DIM)
    s = jnp.einsum('bqkgd,bskd->bkgqs', qg, k, preferred_element_type=jnp.float32) * HEAD_DIM ** -0.5
    cqg = jnp.moveaxis(cq.reshape(B, Tq, N_KV, GROUP), 1, -1)
    ckg = jnp.moveaxis(ck.reshape(B, ck.shape[1], N_KV, GROUP), 1, -1)
    s = s + cqg[..., :, None] - ckg[..., None, :]
    s = jnp.where(kpos[None, :] <= qpos[:, None], s, -jnp.inf)
    p = jax.nn.softmax(s, axis=-1)
    o = jnp.einsum('bkgqs,bskd->bqkgd', p.astype(v.dtype), v)
    return o.reshape(B, Tq, Q_W)


def sink_softmax(s, valid, sink):
    s = jnp.where(valid, s, -jnp.inf)
    m = jnp.maximum(jnp.max(s, axis=-1, keepdims=True), sink)
    e = jnp.exp(s - m)
    return e / (jnp.sum(e, axis=-1, keepdims=True) + jnp.exp(sink - m))


def swa_banded(q, k, v, sink):
    B, T = q.shape[:2]
    nb = T // Q_BLOCK
    qb = q.reshape(B, nb, Q_BLOCK, N_KV, GROUP, HEAD_DIM)
    def with_prev(a):
        ab = a.reshape(B, nb, Q_BLOCK, *a.shape[2:])
        prev = jnp.pad(ab, ((0, 0), (1, 0)) + ((0, 0),) * (ab.ndim - 2))[:, :nb]
        return jnp.concatenate([prev, ab], axis=2)
    kc, vc = with_prev(k), with_prev(v)
    s = jnp.einsum('bnqkgd,bnskd->bnkgqs', qb, kc, preferred_element_type=jnp.float32) * HEAD_DIM ** -0.5
    qi = jnp.arange(Q_BLOCK)
    kj = jnp.arange(2 * Q_BLOCK) - Q_BLOCK
    rel = qi[:, None] - kj[None, :]
    kabs = (jnp.arange(nb) * Q_BLOCK)[:, None] + kj[None, :]
    valid = ((rel >= 0) & (rel <= WINDOW))[None] & (kabs >= 0)[:, None, :]
    p = sink_softmax(s, valid[None, :, None, None], sink)
    o = jnp.einsum('bnkgqs,bnskd->bnqkgd', p.astype(v.dtype), vc)
    return o.reshape(B, T, Q_W)


def mixer_sb(h, w_qkv, w_o, past):
    B, T, _ = h.shape
    q, k, v = split_qkv(h @ w_qkv, B, T)
    if past is None:
        pos = jnp.arange(T)
        o = over_query_blocks(lambda qb, pb: stick_breaking_attend(qb, pb, k, v, pos), (q,), pos)
    else:
        pk, pv = past
        qpos = PAST_LEN + jnp.arange(T)
        kpos = jnp.arange(PAST_LEN + T)
        o = stick_breaking_attend(q, qpos, jnp.concatenate([pk, k], 1), jnp.concatenate([pv, v], 1), kpos)
    return o @ w_o, (k, v)


def mixer_mla(h, w_in, g_q, g_kv, w_uq, w_uk, w_uv, w_o, past):
    B, T, _ = h.shape
    pos = jnp.arange(T) if past is None else PAST_LEN + jnp.arange(T)
    a = h @ w_in
    cq = rmsnorm(a[..., :MLA_Q_RANK], g_q)
    ckv = rmsnorm(a[..., MLA_Q_RANK:MLA_Q_RANK + MLA_KV_RANK], g_kv)
    kr = rope(a[..., MLA_Q_RANK + MLA_KV_RANK:][:, :, None, :], pos)[:, :, 0]
    qf = (cq @ w_uq).reshape(B, T, N_HEADS, MLA_NOPE + MLA_ROPE)
    q_lat = jnp.einsum('bthn,chn->bthc', qf[..., :MLA_NOPE], w_uk)
    q_rope = rope(qf[..., MLA_NOPE:], pos)
    if past is None:
        o = over_query_blocks(lambda ql, qr, pb: mla_attend(ql, qr, pb, ckv, kr, pos, w_uv), (q_lat, q_rope), pos)
    else:
        pc, pr = past
        kpos = jnp.arange(PAST_LEN + T)
        o = mla_attend(q_lat, q_rope, pos, jnp.concatenate([pc, ckv], 1), jnp.concatenate([pr, kr], 1), kpos, w_uv)
    return o.reshape(B, T, N_HEADS * MLA_V) @ w_o, (ckv, kr)


def mixer_fox(h, w_in, b_f, w_o, past):
    B, T, _ = h.shape
    a = h @ w_in
    q, k, v = split_qkv(a[..., :QKV_W], B, T)
    lf = jax.nn.log_sigmoid(a[..., QKV_W:].astype(jnp.float32) + b_f.astype(jnp.float32))
    if past is None:
        pos = jnp.arange(T)
        c = jnp.cumsum(lf, axis=1)
        o = over_query_blocks(lambda qb, cb, pb: fox_attend(qb, cb, pb, k, v, c, pos), (q, c), pos)
    else:
        pk, pv, plf = past
        c = jnp.cumsum(jnp.concatenate([plf.astype(jnp.float32), lf], 1), axis=1)
        qpos = PAST_LEN + jnp.arange(T)
        kpos = jnp.arange(PAST_LEN + T)
        o = fox_attend(q, c[:, PAST_LEN:], qpos, jnp.concatenate([pk, k], 1), jnp.concatenate([pv, v], 1), c, kpos)
    return o @ w_o, (k, v, lf.astype(h.dtype))


def mixer_swa(h, w_qkv, sinks, w_o, past):
    B, T, _ = h.shape
    q, k, v = split_qkv(h @ w_qkv, B, T)
    sink = sinks.astype(jnp.float32).reshape(N_KV, GROUP, 1, 1)
    if past is None:
        pos = jnp.arange(T)
        q, k = rope(q, pos), rope(k, pos)
        o = swa_banded(q, k, v, sink)
        k_all, v_all = k, v
    else:
        bk, bv = past
        qpos = PAST_LEN + jnp.arange(T)
        q, k = rope(q, qpos), rope(k, qpos)
        k_all = jnp.concatenate([bk, k], 1)
        v_all = jnp.concatenate([bv, v], 1)
        kpos = jnp.concatenate([PAST_LEN - WIN_BUF + jnp.arange(WIN_BUF), qpos])
        rel = qpos[:, None] - kpos[None, :]
        valid = (rel >= 0) & (rel <= WINDOW)
        qg = q.reshape(B, T, N_KV, GROUP, HEAD_DIM)
        s = jnp.einsum('bqkgd,bskd->bkgqs', qg, k_all, preferred_element_type=jnp.float32) * HEAD_DIM ** -0.5
        p = sink_softmax(s, valid, sink)
        o = jnp.einsum('bkgqs,bskd->bqkgd', p.astype(v_all.dtype), v_all).reshape(B, T, Q_W)
    return o @ w_o, (k_all[:, -WIN_BUF:], v_all[:, -WIN_BUF:])


def mem_kv(mem, g, w_kv):
    B = mem.shape[0]
    a = rmsnorm(mem, g) @ w_kv
    mk = a[..., :MEM_W].reshape(B, N_MEM, MEM_HEADS, MEM_HEAD_DIM)
    mv = a[..., MEM_W:].reshape(B, N_MEM, MEM_HEADS, MEM_HEAD_DIM)
    return mk, mv


def cross_attend(h, mk, mv, w_q, w_o):
    B, T, _ = h.shape
    q = (h @ w_q).reshape(B, T, MEM_HEADS, MEM_HEAD_DIM)
    s = jnp.einsum('bqhd,bmhd->bhqm', q, mk, preferred_element_type=jnp.float32) * MEM_HEAD_DIM ** -0.5
    p = jax.nn.softmax(s, axis=-1)
    o = jnp.einsum('bhqm,bmhd->bqhd', p.astype(mv.dtype), mv)
    return o.reshape(B, T, MEM_W) @ w_o


def conv_ffn(h, buf, w_in, w_conv, b_conv, w_out):
    T = h.shape[1]
    a = h @ w_in
    gate, val = a[..., :D_FF], a[..., D_FF:]
    ext = jnp.concatenate([buf.astype(gate.dtype), gate], axis=1)
    conv = b_conv
    for j in range(CONV_W):
        conv = conv + w_conv[j] * ext[:, j:j + T]
    y = jax.nn.gelu(conv, approximate=False) * val
    return y @ w_out, ext[:, T:]


def setup_inputs(seed: int = 0) -> dict:
    key = jax.random.key(seed)
    ks = iter(jax.random.split(key, 64))
    f32 = jnp.float32
    def nrm(shape, scale=1.0):
        return jax.random.normal(next(ks), shape, f32) * scale
    def lin(shape, fan_in):
        return nrm(shape, fan_in ** -0.5)
    def gain(shape):
        return 1.0 + nrm(shape, 0.05)
    nA, nB, nC, nD = N_OF
    inp = {}
    inp['x_prompt'] = nrm((BATCH, SEQ, D_MODEL))
    inp['x_sample'] = nrm((DEC_BATCH, DEC_SEQ, D_MODEL))
    inp['cache_sb_k'] = nrm((nA, N_POOL, PAGE_SIZE, N_KV, HEAD_DIM))
    inp['cache_sb_v'] = nrm((nA, N_POOL, PAGE_SIZE, N_KV, HEAD_DIM))
    inp['cache_mla_ckv'] = nrm((nB, N_POOL, PAGE_SIZE, MLA_KV_RANK))
    inp['cache_mla_kr'] = nrm((nB, N_POOL, PAGE_SIZE, MLA_ROPE))
    inp['cache_fox_k'] = nrm((nC, N_POOL, PAGE_SIZE, N_KV, HEAD_DIM))
    inp['cache_fox_v'] = nrm((nC, N_POOL, PAGE_SIZE, N_KV, HEAD_DIM))
    inp['cache_fox_logf'] = jax.nn.log_sigmoid(FORGET_BIAS + nrm((nC, N_POOL, PAGE_SIZE, N_HEADS)))
    inp['state_swa_k'] = nrm((nD, DEC_BATCH, WIN_BUF, N_KV, HEAD_DIM))
    inp['state_swa_v'] = nrm((nD, DEC_BATCH, WIN_BUF, N_KV, HEAD_DIM))
    inp['cache_mem_k'] = nrm((DEPTH, DEC_BATCH, N_MEM, MEM_HEADS, MEM_HEAD_DIM))
    inp['cache_mem_v'] = nrm((DEPTH, DEC_BATCH, N_MEM, MEM_HEADS, MEM_HEAD_DIM))
    inp['state_ffn_conv'] = nrm((DEPTH, DEC_BATCH, CONV_W - 1, D_FF))
    perm = jax.random.permutation(next(ks), N_POOL)[:DEC_BATCH * N_PAGES]
    inp['page_table'] = perm.reshape(DEC_BATCH, N_PAGES).astype(jnp.int32)
    inp['mem_prompt'] = nrm((BATCH, N_MEM, D_MODEL))
    inp['g_mix'] = gain((DEPTH, D_MODEL))
    inp['w_sb_qkv'] = lin((nA, D_MODEL, QKV_W), D_MODEL)
    inp['w_sb_o'] = lin((nA, Q_W, D_MODEL), Q_W)
    inp['w_mla_in'] = lin((nB, D_MODEL, MLA_IN_W), D_MODEL)
    inp['g_mla_q'] = gain((nB, MLA_Q_RANK))
    inp['g_mla_kv'] = gain((nB, MLA_KV_RANK))
    inp['w_mla_uq'] = lin((nB, MLA_Q_RANK, N_HEADS * (MLA_NOPE + MLA_ROPE)), MLA_Q_RANK)
    inp['w_mla_uk'] = lin((nB, MLA_KV_RANK, N_HEADS, MLA_NOPE), MLA_KV_RANK)
    inp['w_mla_uv'] = lin((nB, MLA_KV_RANK, N_HEADS, MLA_V), MLA_KV_RANK)
    inp['w_mla_o'] = lin((nB, N_HEADS * MLA_V, D_MODEL), N_HEADS * MLA_V)
    inp['w_fox_in'] = lin((nC, D_MODEL, QKV_W + N_HEADS), D_MODEL)
    inp['b_fox_f'] = FORGET_BIAS + nrm((nC, N_HEADS), 0.1)
    inp['w_fox_o'] = lin((nC, Q_W, D_MODEL), Q_W)
    inp['w_swa_qkv'] = lin((nD, D_MODEL, QKV_W), D_MODEL)
    inp['swa_sinks'] = nrm((nD, N_HEADS), 0.5)
    inp['w_swa_o'] = lin((nD, Q_W, D_MODEL), Q_W)
    inp['g_cross'] = gain((DEPTH, D_MODEL))
    inp['g_mem'] = gain((DEPTH, D_MODEL))
    inp['w_cq'] = lin((DEPTH, D_MODEL, MEM_W), D_MODEL)
    inp['w_ckv'] = lin((DEPTH, D_MODEL, 2 * MEM_W), D_MODEL)
    inp['w_co'] = lin((DEPTH, MEM_W, D_MODEL), MEM_W)
    inp['g_ffn'] = gain((DEPTH, D_MODEL))
    inp['w_ffn_in'] = lin((DEPTH, D_MODEL, 2 * D_FF), D_MODEL)
    inp['w_ffn_conv'] = lin((DEPTH, CONV_W, D_FF), CONV_W)
    inp['b_ffn_conv'] = nrm((DEPTH, D_FF), 0.02)
    inp['w_ffn_out'] = lin((DEPTH, D_FF, D_MODEL), D_FF)
    inp['g_final'] = gain((D_MODEL,))
    return inp


def reference(x_prompt, x_sample, cache_sb_k, cache_sb_v, cache_mla_ckv, cache_mla_kr,
              cache_fox_k, cache_fox_v, cache_fox_logf, state_swa_k, state_swa_v,
              cache_mem_k, cache_mem_v, state_ffn_conv, page_table, mem_prompt,
              g_mix, w_sb_qkv, w_sb_o, w_mla_in, g_mla_q, g_mla_kv, w_mla_uq, w_mla_uk, w_mla_uv, w_mla_o,
              w_fox_in, b_fox_f, w_fox_o, w_swa_qkv, swa_sinks, w_swa_o,
              g_cross, g_mem, w_cq, w_ckv, w_co,
              g_ffn, w_ffn_in, w_ffn_conv, b_ffn_conv, w_ffn_out, g_final):

    def trunk(x, sample):
        B = x.shape[0]
        st = {'sb': [], 'mla': [], 'fox': [], 'swa': [], 'mem': [], 'conv': []}
        for i in range(DEPTH):
            m, j = i % N_MIXERS, i // N_MIXERS
            h = rmsnorm(x, g_mix[i])
            if m == 0:
                past = (gather_pages(cache_sb_k[j], page_table), gather_pages(cache_sb_v[j], page_table)) if sample else None
                y, s = mixer_sb(h, w_sb_qkv[j], w_sb_o[j], past)
                st['sb'].append(s)
            elif m == 1:
                past = (gather_pages(cache_mla_ckv[j], page_table), gather_pages(cache_mla_kr[j], page_table)) if sample else None
                y, s = mixer_mla(h, w_mla_in[j], g_mla_q[j], g_mla_kv[j], w_mla_uq[j], w_mla_uk[j], w_mla_uv[j], w_mla_o[j], past)
                st['mla'].append(s)
            elif m == 2:
                past = (gather_pages(cache_fox_k[j], page_table), gather_pages(cache_fox_v[j], page_table),
                        gather_pages(cache_fox_logf[j], page_table)) if sample else None
                y, s = mixer_fox(h, w_fox_in[j], b_fox_f[j], w_fox_o[j], past)
                st['fox'].append(s)
            else:
                past = (state_swa_k[j], state_swa_v[j]) if sample else None
                y, s = mixer_swa(h, w_swa_qkv[j], swa_sinks[j], w_swa_o[j], past)
                st['swa'].append(s)
            x = x + y
            if sample:
                mk, mv = cache_mem_k[i], cache_mem_v[i]
            else:
                mk, mv = mem_kv(mem_prompt, g_mem[i], w_ckv[i])
                st['mem'].append((mk, mv))
            x = x + cross_attend(rmsnorm(x, g_cross[i]), mk, mv, w_cq[i], w_co[i])
            buf = state_ffn_conv[i] if sample else jnp.zeros((B, CONV_W - 1, D_FF), x.dtype)
            y, nbuf = conv_ffn(rmsnorm(x, g_ffn[i]), buf, w_ffn_in[i], w_ffn_conv[i], b_ffn_conv[i], w_ffn_out[i])
            st['conv'].append(nbuf)
            x = x + y
        return rmsnorm(x, g_final), st

    def stk(lst, n):
        return jnp.stack([e[n] for e in lst])

    y_prompt, sp = trunk(x_prompt, False)
    y_sample, ss = trunk(x_sample, True)
    return (y_prompt, y_sample,
            stk(sp['sb'], 0), stk(sp['sb'], 1), stk(ss['sb'], 0), stk(ss['sb'], 1),
            stk(sp['mla'], 0), stk(sp['mla'], 1), stk(ss['mla'], 0), stk(ss['mla'], 1),
            stk(sp['fox'], 0), stk(sp['fox'], 1), stk(sp['fox'], 2),
            stk(ss['fox'], 0), stk(ss['fox'], 1), stk(ss['fox'], 2),
            stk(sp['swa'], 0), stk(sp['swa'], 1), stk(ss['swa'], 0), stk(ss['swa'], 1),
            stk(sp['mem'], 0), stk(sp['mem'], 1),
            jnp.stack(sp['conv']), jnp.stack(ss['conv']))
```

```python
import functools
import math

import numpy as np
import jax
import jax.numpy as jnp
from jax import lax
from jax.experimental import pallas as pl
from jax.experimental.pallas import tpu as pltpu

F32 = jnp.float32
BF16 = jnp.bfloat16

D_MODEL = 1024
N_HEADS = 16
HEAD_DIM = 64
N_KV = 4
GROUP = N_HEADS // N_KV
Q_W = N_HEADS * HEAD_DIM
KV_W = N_KV * HEAD_DIM
QKV_W = Q_W + 2 * KV_W
MLA_Q_RANK = 768
MLA_KV_RANK = 256
MLA_NOPE = 64
MLA_ROPE = 32
MLA_V = 64
MLA_QK = MLA_NOPE + MLA_ROPE
WINDOW = 128
MEM_HEADS = 4
MEM_HEAD_DIM = 128
MEM_W = MEM_HEADS * MEM_HEAD_DIM
D_FF = 2816
CONV_W = 3
ROPE_THETA = 10000.0
EPS = 1e-6
PAGE_SIZE = 128

LANES = 128
SUBLANES = 8
NEG_BIG = -1e30
VMEM_LIMIT = 56 * 1024 * 1024
CUM_CHUNK = 256


def _cparams(*sem):
    return pltpu.CompilerParams(dimension_semantics=sem, vmem_limit_bytes=VMEM_LIMIT)


def _full(shape):
    return pl.BlockSpec(shape, lambda *_: (0,) * len(shape))


def _row_tile(m, pref):
    t = min(m, pref)
    assert m % t == 0, (m, t)
    return t


def _rms(x, g):
    return x * lax.rsqrt(jnp.mean(x * x, axis=-1, keepdims=True) + EPS) * g


def _dot(a, b):
    return jnp.dot(a, b, preferred_element_type=F32)


def _dot_nt(a, b):
    return lax.dot_general(a, b, (((1,), (1,)), ((), ())), preferred_element_type=F32)


def _log_sigmoid(z):
    return jnp.minimum(z, 0.0) - jnp.log1p(jnp.exp(-jnp.abs(z)))


def _rope_slab(x, cos_t, sin_a, sin_b, half):
    return (x * cos_t + pltpu.roll(x, LANES - half, axis=1) * sin_a
            + pltpu.roll(x, half, axis=1) * sin_b)


def _rope_wide(x, cos_t, sin_a, sin_b, half):
    n = x.shape[1] // LANES
    return jnp.concatenate(
        [_rope_slab(x[:, i * LANES:(i + 1) * LANES], cos_t, sin_a, sin_b, half) for i in range(n)],
        axis=1)


def _split2(x):
    a = x.astype(BF16)
    b = (x - a.astype(F32)).astype(BF16)
    return a, b


def _split3(x):
    a = x.astype(BF16)
    r = x - a.astype(F32)
    b = r.astype(BF16)
    c = (r - b.astype(F32)).astype(BF16)
    return a, b, c


def _store_heads(dst_ref, a, n_heads, width, scale=None):
    for h in range(n_heads):
        blk = a[:, h * width:(h + 1) * width]
        if scale is not None:
            blk = blk * scale
        dst_ref[h] = blk.astype(dst_ref.dtype)


def _proj_qkv_body(*refs, rope, fox, nseq_tiles):
    it = iter(refs)
    x_ref, g_ref, w_ref = next(it), next(it), next(it)
    if rope:
        cos_ref, sa_ref, sb_ref = next(it), next(it), next(it)
    if fox:
        bf_ref, tri_ref = next(it), next(it)
    q_ref, k_ref, v_ref, kh_ref, vh_ref = next(it), next(it), next(it), next(it), next(it)
    if fox:
        lf_ref, c_ref, carry_sc = next(it), next(it), next(it)

    h = _rms(x_ref[...], g_ref[...]).astype(BF16)
    a = _dot(h, w_ref[...])
    q = a[:, :Q_W]
    k = a[:, Q_W:Q_W + KV_W]
    v = a[:, Q_W + KV_W:QKV_W]
    if rope:
        cos_t, sin_a, sin_b = cos_ref[...], sa_ref[...], sb_ref[...]
        q = _rope_wide(q, cos_t, sin_a, sin_b, HEAD_DIM // 2)
        k = _rope_wide(k, cos_t, sin_a, sin_b, HEAD_DIM // 2)
    k_ref[...] = k
    v_ref[...] = v
    _store_heads(q_ref, q, N_HEADS, HEAD_DIM, scale=HEAD_DIM ** -0.5)
    _store_heads(kh_ref, k, N_KV, HEAD_DIM)
    _store_heads(vh_ref, v, N_KV, HEAD_DIM)
    if fox:
        @pl.when(pl.program_id(0) % nseq_tiles == 0)
        def _():
            carry_sc[...] = jnp.zeros_like(carry_sc)

        lf = _log_sigmoid(a[:, QKV_W:QKV_W + N_HEADS] + bf_ref[...])
        lf_ref[...] = lf
        tri = tri_ref[...]
        t3 = _split3(lf)
        cs = _dot(tri, t3[0]) + _dot(tri, t3[1]) + _dot(tri, t3[2]) + carry_sc[...]
        c_ref[...] = cs
        carry_sc[...] = cs[cs.shape[0] - 1:, :]


def _proj_qkv(x, g, w, *, rope_tabs=None, b_f=None, seq_len=1):
    m = x.shape[0]
    rope = rope_tabs is not None
    fox = b_f is not None
    tm = _row_tile(min(m, seq_len) if fox and seq_len > 1 else m, 512)
    nseq_tiles = max(seq_len // tm, 1)
    in_specs = [pl.BlockSpec((tm, D_MODEL), lambda i: (i, 0)), _full((1, D_MODEL)), _full(w.shape)]
    args = [x, g.reshape(1, D_MODEL), w]
    if rope:
        nt = rope_tabs[0].shape[0] // tm
        for t in rope_tabs:
            in_specs.append(pl.BlockSpec((tm, LANES), lambda i, nt=nt: (i % nt, 0)))
            args.append(t)
    if fox:
        in_specs += [_full((1, N_HEADS)), _full((tm, tm))]
        args += [b_f.reshape(1, N_HEADS), jnp.tril(jnp.ones((tm, tm), F32)).astype(BF16)]
    out_shape = [
        jax.ShapeDtypeStruct((N_HEADS, m, HEAD_DIM), BF16),
        jax.ShapeDtypeStruct((m, KV_W), F32),
        jax.ShapeDtypeStruct((m, KV_W), F32),
        jax.ShapeDtypeStruct((N_KV, m, HEAD_DIM), BF16),
        jax.ShapeDtypeStruct((N_KV, m, HEAD_DIM), BF16),
    ]
    out_specs = [
        pl.BlockSpec((N_HEADS, tm, HEAD_DIM), lambda i: (0, i, 0)),
        pl.BlockSpec((tm, KV_W), lambda i: (i, 0)),
        pl.BlockSpec((tm, KV_W), lambda i: (i, 0)),
        pl.BlockSpec((N_KV, tm, HEAD_DIM), lambda i: (0, i, 0)),
        pl.BlockSpec((N_KV, tm, HEAD_DIM), lambda i: (0, i, 0)),
    ]
    scratch = []
    if fox:
        out_shape += [jax.ShapeDtypeStruct((m, N_HEADS), F32)] * 2
        out_specs += [pl.BlockSpec((tm, N_HEADS), lambda i: (i, 0))] * 2
        scratch = [pltpu.VMEM((1, N_HEADS), F32)]
    return pl.pallas_call(
        functools.partial(_proj_qkv_body, rope=rope, fox=fox, nseq_tiles=nseq_tiles),
        grid=(m // tm,), in_specs=in_specs, out_specs=out_specs, out_shape=out_shape,
        scratch_shapes=scratch, compiler_params=_cparams("arbitrary"),
        name="proj_fox" if fox else ("proj_swa" if rope else "proj_sb"),
    )(*args)


def _proj_mla_body(x_ref, g_ref, win_ref, gq_ref, gkv_ref, wuq_ref, wuk_ref, wuv_ref, wukt_ref,
                   cos_ref, sa_ref, sb_ref,
                   q_ref, kh_ref, v_ref, ckv_ref, kr_ref, *maybe_qlat, absorbed):
    scale = MLA_QK ** -0.5
    half = MLA_ROPE // 2
    cos_t, sin_a, sin_b = cos_ref[...], sa_ref[...], sb_ref[...]
    h = _rms(x_ref[...], g_ref[...]).astype(BF16)
    a = _dot(h, win_ref[...])
    cq = _rms(a[:, :MLA_Q_RANK], gq_ref[...]).astype(BF16)
    ckv = _rms(a[:, MLA_Q_RANK:MLA_Q_RANK + MLA_KV_RANK], gkv_ref[...])
    ckv_ref[...] = ckv
    ckv_b = ckv.astype(BF16)
    kr = _rope_slab(a[:, MLA_Q_RANK + MLA_KV_RANK:], cos_t, sin_a, sin_b, half)
    kr_ref[...] = kr
    qf = _dot(cq, wuq_ref[...])
    kn = _dot(ckv_b, wuk_ref[...])
    v_ref[...] = _dot(ckv_b, wuv_ref[...]).astype(BF16)
    for hd in range(N_HEADS):
        sl = slice(hd * LANES, (hd + 1) * LANES)
        qh = _rope_slab(qf[:, sl], cos_t, sin_a, sin_b, half) * scale
        q_ref[hd] = qh.astype(BF16)
        kh_ref[hd] = (kn[:, sl] + kr).astype(BF16)
        if absorbed:
            qn = qh[:, :MLA_NOPE].astype(BF16)
            maybe_qlat[0][hd] = _dot(qn, wukt_ref[hd])


def _proj_mla(x, g, wp, rope_tabs, *, absorbed):
    m = x.shape[0]
    tm = _row_tile(m, 512)
    nt = rope_tabs[0].shape[0] // tm
    args = [x, g.reshape(1, D_MODEL), wp["w_in"], wp["g_q"], wp["g_kv"], wp["w_uq"], wp["w_uk"],
            wp["w_uv"], wp["w_ukt"]]
    in_specs = [pl.BlockSpec((tm, D_MODEL), lambda i: (i, 0))] + [_full(a.shape) for a in args[1:]]
    for t in rope_tabs:
        in_specs.append(pl.BlockSpec((tm, LANES), lambda i, nt=nt: (i % nt, 0)))
        args.append(t)
    out_shape = [
        jax.ShapeDtypeStruct((N_HEADS, m, LANES), BF16),
        jax.ShapeDtypeStruct((N_HEADS, m, LANES), BF16),
        jax.ShapeDtypeStruct((m, N_HEADS * MLA_V), BF16),
        jax.ShapeDtypeStruct((m, MLA_KV_RANK), F32),
        jax.ShapeDtypeStruct((m, LANES), F32),
    ]
    out_specs = [
        pl.BlockSpec((N_HEADS, tm, LANES), lambda i: (0, i, 0)),
        pl.BlockSpec((N_HEADS, tm, LANES), lambda i: (0, i, 0)),
        pl.BlockSpec((tm, N_HEADS * MLA_V), lambda i: (i, 0)),
        pl.BlockSpec((tm, MLA_KV_RANK), lambda i: (i, 0)),
        pl.BlockSpec((tm, LANES), lambda i: (i, 0)),
    ]
    if absorbed:
        out_shape.append(jax.ShapeDtypeStruct((N_HEADS, m, MLA_KV_RANK), F32))
        out_specs.append(pl.BlockSpec((N_HEADS, tm, MLA_KV_RANK), lambda i: (0, i, 0)))
    return pl.pallas_call(
        functools.partial(_proj_mla_body, absorbed=absorbed),
        grid=(m // tm,), in_specs=in_specs, out_specs=out_specs, out_shape=out_shape,
        compiler_params=_cparams("arbitrary"), name="proj_mla",
    )(*args)


def _proj_plain_body(x_ref, g_ref, w_ref, o_ref, ob_ref):
    a = _dot(_rms(x_ref[...], g_ref[...]).astype(BF16), w_ref[...])
    o_ref[...] = a
    ob_ref[...] = a.astype(BF16)


def _proj_plain(x, g, w):
    m, n = x.shape[0], w.shape[1]
    tm = _row_tile(m, 512)
    return pl.pallas_call(
        _proj_plain_body, grid=(m // tm,),
        in_specs=[pl.BlockSpec((tm, D_MODEL), lambda i: (i, 0)), _full((1, D_MODEL)), _full(w.shape)],
        out_specs=[pl.BlockSpec((tm, n), lambda i: (i, 0))] * 2,
        out_shape=[jax.ShapeDtypeStruct((m, n), F32), jax.ShapeDtypeStruct((m, n), BF16)],
        compiler_params=_cparams("arbitrary"), name="proj_mem",
    )(x, g.reshape(1, D_MODEL), w)


FLAG_FIRST, FLAG_LAST, FLAG_MASK = 1, 2, 4


def _schedule(n_tiles, tile, *, window=None, descending=False, strict=False):
    qi, kj, fl = [], [], []
    for i in range(n_tiles):
        q_lo, q_hi = i * tile, (i + 1) * tile - 1
        lo = 0 if window is None else max(0, (q_lo - window) // tile)
        js = list(range(lo, i + 1))
        if descending:
            js = js[::-1]
        for n, j in enumerate(js):
            k_hi = (j + 1) * tile - 1
            need_mask = (k_hi >= q_lo) if strict else (k_hi > q_lo)
            if window is not None:
                need_mask = True
            f = (FLAG_FIRST if n == 0 else 0) | (FLAG_LAST if n == len(js) - 1 else 0)
            f |= FLAG_MASK if need_mask else 0
            qi.append(i), kj.append(j), fl.append(f)
    return (jnp.asarray(qi, jnp.int32), jnp.asarray(kj, jnp.int32), jnp.asarray(fl, jnp.int32))


def _attn_body(qi_ref, kj_ref, fl_ref, *refs, mode, nh, tq, tk):
    it = iter(refs)
    if mode == "swa":
        sink_ref = next(it)
    q_ref, k_ref, v_ref = next(it), next(it), next(it)
    if mode == "fox":
        cq_ref, ck_ref = next(it), next(it)
    o_ref, m_sc, l_sc, acc_sc = next(it), next(it), next(it), next(it)
    g = pl.program_id(1)
    s_id = pl.program_id(2)
    flag = fl_ref[s_id]
    q_lo = qi_ref[s_id] * tq
    k_lo = kj_ref[s_id] * tk

    @pl.when((flag & FLAG_FIRST) != 0)
    def _():
        for hh in range(nh):
            if mode == "swa":
                m_sc[hh] = jnp.full((tq, 1), sink_ref[g * nh + hh], F32)
                l_sc[hh] = jnp.ones((tq, 1), F32)
            else:
                m_sc[hh] = jnp.full((tq, 1), NEG_BIG, F32)
                l_sc[hh] = jnp.zeros((tq, 1), F32)
        acc_sc[...] = jnp.zeros_like(acc_sc)

    def step(masked):
        if masked:
            qpos = q_lo + lax.broadcasted_iota(jnp.int32, (tq, tk), 0)
            kpos = k_lo + lax.broadcasted_iota(jnp.int32, (tq, tk), 1)
            vis = kpos <= qpos
            if mode == "swa":
                vis = vis & (qpos - kpos <= WINDOW)
        for hh in range(nh):
            q = q_ref[hh]
            k = k_ref[hh] if mode == "mla" else k_ref[0]
            v = v_ref[...] if mode == "mla" else v_ref[0]
            s = _dot_nt(q, k)
            if mode == "fox":
                s = s + (cq_ref[0][:, hh:hh + 1] - ck_ref[0][hh:hh + 1, :])
            if masked:
                s = jnp.where(vis, s, NEG_BIG)
            m_prev = m_sc[hh]
            m_new = jnp.maximum(m_prev, jnp.max(s, axis=-1, keepdims=True))
            alpha = jnp.exp(m_prev - m_new)
            p = jnp.exp(s - m_new)
            l_sc[hh] = alpha * l_sc[hh] + jnp.sum(p, axis=-1, keepdims=True)
            acc_sc[hh] = alpha * acc_sc[hh] + _dot(p.astype(BF16), v)
            m_sc[hh] = m_new

    if mode == "swa":
        step(True)
    else:
        @pl.when((flag & FLAG_MASK) != 0)
        def _():
            step(True)

        @pl.when((flag & FLAG_MASK) == 0)
        def _():
            step(False)

    @pl.when((flag & FLAG_LAST) != 0)
    def _():
        if mode == "mla":
            lane_head = lax.broadcasted_iota(jnp.int32, (tq, nh * MLA_V), 1) // MLA_V
            out = jnp.zeros((tq, nh * MLA_V), F32)
            for hh in range(nh):
                out = out + jnp.where(lane_head == hh, acc_sc[hh] / l_sc[hh], 0.0)
            o_ref[...] = out.astype(o_ref.dtype)
        else:
            for hh in range(nh):
                o_ref[:, hh * HEAD_DIM:(hh + 1) * HEAD_DIM] = (acc_sc[hh] / l_sc[hh]).astype(o_ref.dtype)


def _attn(mode, q_hm, k_hm, v, batch, seq, *, cq=None, ck=None, sinks=None):
    nh = GROUP
    tq = tk = _row_tile(seq, 512)
    nq = seq // tq
    dk = q_hm.shape[-1]
    window = WINDOW if mode == "swa" else None
    qi, kj, fl = _schedule(nq, tq, window=window)
    prefetch = [qi, kj, fl]
    if mode == "swa":
        prefetch.append(sinks.astype(F32))
    npf = len(prefetch)

    def qmap(b, g, s, qi, kj, *_):
        return (g, b * nq + qi[s], 0)

    def kmap(b, g, s, qi, kj, *_):
        return (g, b * nq + kj[s], 0)

    in_specs = [pl.BlockSpec((nh, tq, dk), qmap)]
    if mode == "mla":
        in_specs.append(pl.BlockSpec((nh, tk, dk), kmap))
        in_specs.append(pl.BlockSpec((tk, nh * MLA_V), lambda b, g, s, qi, kj, *_: (b * nq + kj[s], g)))
        dacc = nh * MLA_V
    else:
        in_specs.append(pl.BlockSpec((1, tk, dk), kmap))
        in_specs.append(pl.BlockSpec((1, tk, HEAD_DIM), kmap))
        dacc = HEAD_DIM
    args = [q_hm, k_hm, v]
    if mode == "fox":
        in_specs.append(pl.BlockSpec((1, tq, GROUP), qmap))
        in_specs.append(pl.BlockSpec((1, GROUP, tk), lambda b, g, s, qi, kj, *_: (g, 0, b * nq + kj[s])))
        args += [cq, ck]
    grid_spec = pltpu.PrefetchScalarGridSpec(
        num_scalar_prefetch=npf, grid=(batch, N_HEADS // nh, int(qi.shape[0])),
        in_specs=in_specs,
        out_specs=pl.BlockSpec((tq, nh * HEAD_DIM), lambda b, g, s, qi, kj, *_: (b * nq + qi[s], g)),
        scratch_shapes=[pltpu.VMEM((nh, tq, 1), F32), pltpu.VMEM((nh, tq, 1), F32),
                        pltpu.VMEM((nh, tq, dacc), F32)])
    return pl.pallas_call(
        functools.partial(_attn_body, mode=mode, nh=nh, tq=tq, tk=tk),
        grid_spec=grid_spec, out_shape=jax.ShapeDtypeStruct((batch * seq, Q_W), BF16),
        compiler_params=_cparams("arbitrary", "arbitrary", "arbitrary"), name="attn_" + mode,
    )(*prefetch, *args)


def _attn_sb_body(qi_ref, kj_ref, fl_ref, q_ref, k_ref, v_ref, u_ref, o_ref, run_sc, acc_sc, *, nh, tq, tk):
    s_id = pl.program_id(2)
    flag = fl_ref[s_id]
    q_lo = qi_ref[s_id] * tq
    k_lo = kj_ref[s_id] * tk
    ck = min(CUM_CHUNK, tk)
    n_chunks = tk // ck

    @pl.when((flag & FLAG_FIRST) != 0)
    def _():
        run_sc[...] = jnp.zeros_like(run_sc)
        acc_sc[...] = jnp.zeros_like(acc_sc)

    def step(masked):
        u = u_ref[...]
        for r in reversed(range(n_chunks)):
            kc = k_ref[0, r * ck:(r + 1) * ck, :]
            vc = v_ref[0, r * ck:(r + 1) * ck, :]
            if masked:
                qpos = q_lo + lax.broadcasted_iota(jnp.int32, (tq, ck), 0)
                kpos = k_lo + r * ck + lax.broadcasted_iota(jnp.int32, (tq, ck), 1)
                vis = kpos < qpos
            for hh in range(nh):
                z = _dot_nt(q_ref[hh], kc)
                ls = _log_sigmoid(z)
                lr = ls - z
                if masked:
                    lr = jnp.where(vis, lr, 0.0)
                hi, lo = _split2(lr)
                tail = _dot(hi, u) + _dot(lo, u)
                run = run_sc[hh]
                w = jnp.exp(ls + tail + run)
                if masked:
                    w = jnp.where(vis, w, 0.0)
                acc_sc[hh] = acc_sc[hh] + _dot(w.astype(BF16), vc)
                run_sc[hh] = run + jnp.sum(lr, axis=-1, keepdims=True)

    @pl.when((flag & FLAG_MASK) != 0)
    def _():
        step(True)

    @pl.when((flag & FLAG_MASK) == 0)
    def _():
        step(False)

    @pl.when((flag & FLAG_LAST) != 0)
    def _():
        for hh in range(nh):
            o_ref[:, hh * HEAD_DIM:(hh + 1) * HEAD_DIM] = acc_sc[hh].astype(o_ref.dtype)


def _attn_sb(q_hm, k_hm, v_hm, batch, seq):
    nh = GROUP
    tq = tk = _row_tile(seq, 512)
    nq = seq // tq
    ck = min(CUM_CHUNK, tk)
    qi, kj, fl = _schedule(nq, tq, descending=True, strict=True)
    u = (jnp.arange(ck)[:, None] > jnp.arange(ck)[None, :]).astype(BF16)

    def qmap(b, g, s, qi, kj, fl):
        return (g, b * nq + qi[s], 0)

    def kmap(b, g, s, qi, kj, fl):
        return (g, b * nq + kj[s], 0)

    grid_spec = pltpu.PrefetchScalarGridSpec(
        num_scalar_prefetch=3, grid=(batch, N_KV, int(qi.shape[0])),
        in_specs=[pl.BlockSpec((nh, tq, HEAD_DIM), qmap), pl.BlockSpec((1, tk, HEAD_DIM), kmap),
                  pl.BlockSpec((1, tk, HEAD_DIM), kmap), pl.BlockSpec((ck, ck), lambda *_: (0, 0))],
        out_specs=pl.BlockSpec((tq, nh * HEAD_DIM), lambda b, g, s, qi, kj, fl: (b * nq + qi[s], g)),
        scratch_shapes=[pltpu.VMEM((nh, tq, 1), F32), pltpu.VMEM((nh, tq, HEAD_DIM), F32)])
    return pl.pallas_call(
        functools.partial(_attn_sb_body, nh=nh, tq=tq, tk=tk),
        grid_spec=grid_spec, out_shape=jax.ShapeDtypeStruct((batch * seq, Q_W), BF16),
        compiler_params=_cparams("arbitrary", "arbitrary", "arbitrary"), name="attn_sb",
    )(qi, kj, fl, q_hm, k_hm, v_hm, u)


def _suffix_sum_lanes(x):
    h, w = x.shape
    n = w // LANES
    lane = lax.broadcasted_iota(jnp.int32, (h, LANES), 1)
    slabs = []
    for i in range(n):
        y = x[:, i * LANES:(i + 1) * LANES]
        d = 1
        while d < LANES:
            y = y + jnp.where(lane < LANES - d, pltpu.roll(y, LANES - d, axis=1), 0.0)
            d *= 2
        slabs.append(y)
    tot = jnp.zeros((h, 1), F32)
    out = [None] * n
    for i in reversed(range(n)):
        out[i] = slabs[i] + tot
        tot = tot + slabs[i][:, 0:1]
    return (jnp.concatenate(out, axis=1) if n > 1 else out[0]), tot


def _decode_body(pt_ref, *refs, mode, pb, n_kv, gq, hq, dv, scale, new_key):
    it = iter(refs)
    q_ref = next(it)
    qr_ref = next(it) if mode == "mla" else None
    k_refs = [next(it) for _ in range(pb)]
    kr_refs = [next(it) for _ in range(pb)] if mode == "mla" else None
    v_refs = k_refs if mode == "mla" else [next(it) for _ in range(pb)]
    lf_refs = [next(it) for _ in range(pb)] if mode == "fox" else None
    if new_key:
        kn_ref = next(it)
        krn_ref = next(it) if mode == "mla" else None
        vn_ref = kn_ref if mode == "mla" else next(it)
        lfn_ref = next(it) if mode == "fox" else None
    sink_ref = next(it) if mode == "swa" else None
    o_ref, m_sc, l_sc, acc_sc, car_sc = next(it), next(it), next(it), next(it), next(it)
    t = pl.program_id(1)
    dk = q_ref.shape[-1]
    dvt = acc_sc.shape[-1]
    softmax = mode != "sb"

    if n_kv > 1:
        row_kv = lax.broadcasted_iota(jnp.int32, (hq, dk), 0) // gq
        lane_kv = lax.broadcasted_iota(jnp.int32, (hq, dk), 1) // (dk // n_kv)
        q = jnp.where(row_kv == lane_kv, q_ref[0] * scale, 0.0)
    else:
        q = q_ref[0] * scale
    qb = q.astype(BF16)

    @pl.when(t == 0)
    def _():
        if mode == "swa":
            m_sc[...] = sink_ref[...]
            l_sc[...] = jnp.ones_like(l_sc)
        else:
            m_sc[...] = jnp.full(m_sc.shape, NEG_BIG, F32)
            l_sc[...] = jnp.zeros_like(l_sc)
        acc_sc[...] = jnp.zeros_like(acc_sc)
        car_sc[...] = lfn_ref[0] if (mode == "fox" and new_key) else jnp.zeros_like(car_sc)

    order = list(reversed(range(pb)))
    s_parts = []
    for r in order:
        kp = k_refs[r][0].astype(BF16)
        s = _dot_nt(qb, kp)
        if mode == "mla":
            s = s + _dot_nt((qr_ref[0] * scale).astype(BF16), kr_refs[r][0].astype(BF16))
        s_parts.append(s)
    s = jnp.concatenate(s_parts, axis=1) if pb > 1 else s_parts[0]
    ps = s.shape[1] // pb

    def pv(p):
        pb16 = p.astype(BF16)
        out = None
        for n, r in enumerate(order):
            term = _dot(pb16[:, n * ps:(n + 1) * ps], v_refs[r][0].astype(BF16))
            out = term if out is None else out + term
        return out

    if mode == "sb":
        ls = _log_sigmoid(s)
        lr = ls - s
        incl, tot = _suffix_sum_lanes(lr)
        w = jnp.exp(ls + (incl - lr) + car_sc[...])
        acc_sc[...] = acc_sc[...] + pv(w)
        car_sc[...] = car_sc[...] + tot
    else:
        if mode == "fox":
            lf = jnp.concatenate([lf_refs[r][0] for r in order], axis=1) if pb > 1 else lf_refs[0][0]
            incl, tot = _suffix_sum_lanes(lf)
            s = s + (incl - lf) + car_sc[...]
            car_sc[...] = car_sc[...] + tot
        m_prev = m_sc[...]
        m_new = jnp.maximum(m_prev, jnp.max(s, axis=-1, keepdims=True))
        alpha = jnp.exp(m_prev - m_new)
        p = jnp.exp(s - m_new)
        l_sc[...] = alpha * l_sc[...] + jnp.sum(p, axis=-1, keepdims=True)
        acc_sc[...] = alpha * acc_sc[...] + pv(p)
        m_sc[...] = m_new

    @pl.when(t == pl.num_programs(1) - 1)
    def _():
        acc = acc_sc[...]
        if softmax:
            m_prev, l_prev = m_sc[...], l_sc[...]
            if new_key:
                s_new = jnp.sum(q * kn_ref[0], axis=-1, keepdims=True)
                if mode == "mla":
                    s_new = s_new + jnp.sum(qr_ref[0] * scale * krn_ref[0], axis=-1, keepdims=True)
                m_new = jnp.maximum(m_prev, s_new)
                alpha = jnp.exp(m_prev - m_new)
                p_new = jnp.exp(s_new - m_new)
                l_prev = alpha * l_prev + p_new
                acc = alpha * acc + p_new * vn_ref[0]
            acc = acc / l_prev
        if n_kv > 1:
            row_kv = lax.broadcasted_iota(jnp.int32, (hq, dv), 0) // gq
            out = jnp.zeros((hq, dv), F32)
            for kv in range(n_kv):
                out = out + jnp.where(row_kv == kv, acc[:, kv * dv:(kv + 1) * dv], 0.0)
            o_ref[0] = out
        else:
            o_ref[0] = acc


def _decode(mode, q, k_pool, v_pool, page_table, *, n_kv, gq, scale, pb, qr=None, kr_pool=None,
            lf_pool=None, k_new=None, v_new=None, kr_new=None, lf_new=None, sinks=None):
    bsz, hq, dk = q.shape
    n_pages = page_table.shape[1]
    ps = k_pool.shape[1]
    assert n_pages % pb == 0
    dvt = (v_pool if v_pool is not None else k_pool).shape[-1]
    dv = dvt // n_kv
    new_key = k_new is not None
    pt = page_table.reshape(-1).astype(jnp.int32)

    def page_map(r):
        return lambda b, t, pt: (pt[b * n_pages + n_pages - 1 - (t * pb + r)], 0, 0)

    def seq_map(b, t, pt):
        return (b, 0, 0)

    in_specs = [pl.BlockSpec((1, hq, dk), seq_map)]
    args = [q]
    if mode == "mla":
        in_specs.append(pl.BlockSpec((1, hq, qr.shape[-1]), seq_map))
        args.append(qr)
    for r in range(pb):
        in_specs.append(pl.BlockSpec((1, ps, dk), page_map(r)))
        args.append(k_pool)
    if mode == "mla":
        for r in range(pb):
            in_specs.append(pl.BlockSpec((1, ps, kr_pool.shape[-1]), page_map(r)))
            args.append(kr_pool)
    else:
        for r in range(pb):
            in_specs.append(pl.BlockSpec((1, ps, dvt), page_map(r)))
            args.append(v_pool)
    if mode == "fox":
        for r in range(pb):
            in_specs.append(pl.BlockSpec((1, hq, ps), page_map(r)))
            args.append(lf_pool)
    if new_key:
        in_specs.append(pl.BlockSpec((1, 1, dk), seq_map))
        args.append(k_new)
        if mode == "mla":
            in_specs.append(pl.BlockSpec((1, 1, kr_new.shape[-1]), seq_map))
            args.append(kr_new)
        else:
            in_specs.append(pl.BlockSpec((1, 1, dvt), seq_map))
            args.append(v_new)
        if mode == "fox":
            in_specs.append(pl.BlockSpec((1, hq, 1), seq_map))
            args.append(lf_new)
    if mode == "swa":
        in_specs.append(pl.BlockSpec((hq, 1), lambda b, t, pt: (0, 0)))
        args.append(sinks.reshape(hq, 1).astype(F32))
    out_d = dv if n_kv > 1 else dvt
    grid_spec = pltpu.PrefetchScalarGridSpec(
        num_scalar_prefetch=1, grid=(bsz, n_pages // pb), in_specs=in_specs,
        out_specs=pl.BlockSpec((1, hq, out_d), seq_map),
        scratch_shapes=[pltpu.VMEM((hq, 1), F32), pltpu.VMEM((hq, 1), F32),
                        pltpu.VMEM((hq, dvt), F32), pltpu.VMEM((hq, 1), F32)])
    return pl.pallas_call(
        functools.partial(_decode_body, mode=mode, pb=pb, n_kv=n_kv, gq=gq, hq=hq, dv=dv, scale=scale,
                          new_key=new_key),
        grid_spec=grid_spec, out_shape=jax.ShapeDtypeStruct((bsz, hq, out_d), F32),
        compiler_params=_cparams("arbitrary", "arbitrary"), name="decode_" + mode,
    )(pt, *args)


def _heads_matmul_body(x_ref, w_ref, o_ref):
    o_ref[0] = _dot(x_ref[0].astype(BF16), w_ref[0])


def _heads_matmul(x, w):
    h, m, k = x.shape
    n = w.shape[-1]
    return pl.pallas_call(
        _heads_matmul_body, grid=(h,),
        in_specs=[pl.BlockSpec((1, m, k), lambda i: (i, 0, 0)), pl.BlockSpec((1, k, n), lambda i: (i, 0, 0))],
        out_specs=pl.BlockSpec((1, m, n), lambda i: (i, 0, 0)),
        out_shape=jax.ShapeDtypeStruct((h, m, n), F32),
        compiler_params=_cparams("arbitrary"), name="mla_uv",
    )(x, w)


def _oproj_body(x_ref, o_ref, wo_ref, gc_ref, wcq_ref, *refs, cross):
    if cross:
        mk_ref, mv_ref, x1_ref, oc_ref = refs
    else:
        x1_ref, qc_ref = refs
    x1 = x_ref[...] + _dot(o_ref[...].astype(BF16), wo_ref[...])
    x1_ref[...] = x1
    qc = _dot(_rms(x1, gc_ref[...]).astype(BF16), wcq_ref[...])
    if not cross:
        qc_ref[...] = qc
        return
    qc = (qc * MEM_HEAD_DIM ** -0.5).astype(BF16)
    for hh in range(MEM_HEADS):
        sl = slice(hh * MEM_HEAD_DIM, (hh + 1) * MEM_HEAD_DIM)
        s = _dot_nt(qc[:, sl], mk_ref[:, sl])
        p = jnp.exp(s - jnp.max(s, axis=-1, keepdims=True))
        o = _dot(p.astype(BF16), mv_ref[:, sl]) / jnp.sum(p, axis=-1, keepdims=True)
        oc_ref[:, sl] = o.astype(oc_ref.dtype)


def _oproj(x, o, w_o, g_c, w_cq, *, mem=None, seq_len=1):
    m = x.shape[0]
    tm = _row_tile(min(m, seq_len) if mem is not None else m, 512)
    cross = mem is not None
    in_specs = [pl.BlockSpec((tm, D_MODEL), lambda i: (i, 0)), pl.BlockSpec((tm, o.shape[1]), lambda i: (i, 0)),
                _full(w_o.shape), _full((1, D_MODEL)), _full(w_cq.shape)]
    args = [x, o, w_o, g_c.reshape(1, D_MODEL), w_cq]
    if cross:
        n_mem = mem[0].shape[0] // (m // seq_len)
        tiles_per_seq = seq_len // tm
        for a in mem:
            in_specs.append(pl.BlockSpec((n_mem, MEM_W), lambda i: (i // tiles_per_seq, 0)))
            args.append(a)
    out_shape = [jax.ShapeDtypeStruct((m, D_MODEL), F32),
                 jax.ShapeDtypeStruct((m, MEM_W), BF16 if cross else F32)]
    out_specs = [pl.BlockSpec((tm, D_MODEL), lambda i: (i, 0)), pl.BlockSpec((tm, MEM_W), lambda i: (i, 0))]
    return pl.pallas_call(
        functools.partial(_oproj_body, cross=cross), grid=(m // tm,),
        in_specs=in_specs, out_specs=out_specs, out_shape=out_shape,
        compiler_params=_cparams("arbitrary"), name="oproj_cross" if cross else "oproj",
    )(*args)


def _ffn_body(*refs, seq_mode, final, nseq_tiles, tm, fc):
    it = iter(refs)
    x1_ref, oc_ref, wco_ref, gf_ref = next(it), next(it), next(it), next(it)
    wg_ref, wv_ref, wc_ref, bc_ref, wout_ref = next(it), next(it), next(it), next(it), next(it)
    if not seq_mode:
        b0_ref, b1_ref = next(it), next(it)
    if final:
        gfin_ref = next(it)
    x3_ref, gate_ref = next(it), next(it)
    if final:
        y_ref = next(it)
    x2_sc, h_sc, acc_sc = next(it), next(it), next(it)
    if seq_mode:
        halo_sc, gs_sc = next(it), next(it)
    i = pl.program_id(0)
    f = pl.program_id(1)

    @pl.when(f == 0)
    def _():
        x2 = x1_ref[...] + _dot(oc_ref[...].astype(BF16), wco_ref[...])
        x2_sc[...] = x2
        h_sc[...] = _rms(x2, gf_ref[...]).astype(BF16)
        acc_sc[...] = jnp.zeros_like(acc_sc)

    h = h_sc[...]
    gate = _dot(h, wg_ref[...])
    val = _dot(h, wv_ref[...])
    wc = wc_ref[...]
    if seq_mode:
        @pl.when(i % nseq_tiles == 0)
        def _():
            gs_sc[0:SUBLANES, :] = jnp.zeros((SUBLANES, fc), F32)

        @pl.when(i % nseq_tiles != 0)
        def _():
            gs_sc[0:SUBLANES, :] = halo_sc[f]

        gs_sc[SUBLANES:, :] = gate
        gm1 = gs_sc[SUBLANES - 1:SUBLANES - 1 + tm, :]
        gm2 = gs_sc[SUBLANES - 2:SUBLANES - 2 + tm, :]
        tail = gate[tm - SUBLANES:, :]
        halo_sc[f] = tail
        gate_ref[0] = tail
    else:
        gm2, gm1 = b0_ref[...], b1_ref[...]
        gate_ref[...] = gate
    conv = bc_ref[...] + wc[0:1, :] * gm2 + wc[1:2, :] * gm1 + wc[2:3, :] * gate
    y = 0.5 * conv * (1.0 + lax.erf(conv * (2.0 ** -0.5))) * val
    acc_sc[...] = acc_sc[...] + _dot(y.astype(BF16), wout_ref[...])

    @pl.when(f == pl.num_programs(1) - 1)
    def _():
        x3 = x2_sc[...] + acc_sc[...]
        x3_ref[...] = x3
        if final:
            y_ref[...] = _rms(x3, gfin_ref[...])


def _ffn(x1, oc, w_co, g_f, w_in, w_conv, b_conv, w_out, *, seq_len=1, buf=None, g_final=None):
    m = x1.shape[0]
    seq_mode = buf is None
    final = g_final is not None
    tm = _row_tile(min(m, seq_len) if seq_mode else m, 512)
    nseq_tiles = max(seq_len // tm, 1)
    fc = D_FF // 2
    nf = D_FF // fc
    row = lambda i, f: (i, 0)
    in_specs = [pl.BlockSpec((tm, D_MODEL), row), pl.BlockSpec((tm, MEM_W), row), _full(w_co.shape),
                _full((1, D_MODEL)),
                pl.BlockSpec((D_MODEL, fc), lambda i, f: (0, f)),
                pl.BlockSpec((D_MODEL, fc), lambda i, f: (0, nf + f)),
                pl.BlockSpec((CONV_W, fc), lambda i, f: (0, f)),
                pl.BlockSpec((1, fc), lambda i, f: (0, f)),
                pl.BlockSpec((fc, D_MODEL), lambda i, f: (f, 0))]
    args = [x1, oc, w_co, g_f.reshape(1, D_MODEL), w_in, w_in, w_conv, b_conv.reshape(1, D_FF), w_out]
    if not seq_mode:
        in_specs += [pl.BlockSpec((tm, fc), lambda i, f: (i, f))] * 2
        args += [buf[0], buf[1]]
    if final:
        in_specs.append(_full((1, D_MODEL)))
        args.append(g_final.reshape(1, D_MODEL))
    out_shape = [jax.ShapeDtypeStruct((m, D_MODEL), F32)]
    out_specs = [pl.BlockSpec((tm, D_MODEL), row)]
    if seq_mode:
        out_shape.append(jax.ShapeDtypeStruct((m // tm, SUBLANES, D_FF), F32))
        out_specs.append(pl.BlockSpec((1, SUBLANES, fc), lambda i, f: (i, 0, f)))
    else:
        out_shape.append(jax.ShapeDtypeStruct((m, D_FF), F32))
        out_specs.append(pl.BlockSpec((tm, fc), lambda i, f: (i, f)))
    if final:
        out_shape.append(jax.ShapeDtypeStruct((m, D_MODEL), F32))
        out_specs.append(pl.BlockSpec((tm, D_MODEL), row))
    scratch = [pltpu.VMEM((tm, D_MODEL), F32), pltpu.VMEM((tm, D_MODEL), BF16), pltpu.VMEM((tm, D_MODEL), F32)]
    if seq_mode:
        scratch += [pltpu.VMEM((nf, SUBLANES, fc), F32), pltpu.VMEM((tm + SUBLANES, fc), F32)]
    return pl.pallas_call(
        functools.partial(_ffn_body, seq_mode=seq_mode, final=final, nseq_tiles=nseq_tiles, tm=tm, fc=fc),
        grid=(m // tm, nf), in_specs=in_specs, out_specs=out_specs, out_shape=out_shape,
        scratch_shapes=scratch, compiler_params=_cparams("arbitrary", "arbitrary"),
        name="ffn_seq" if seq_mode else "ffn_tok",
    )(*args)


def _rope_tables(pos, half, period, lane_off):
    inv = jnp.exp(-math.log(ROPE_THETA) * jnp.arange(half, dtype=F32) / half)
    ang = pos.astype(F32)[:, None] * inv[None, :]
    cos, sin = jnp.cos(ang), jnp.sin(ang)
    n = pos.shape[0]
    one = jnp.ones((n, lane_off), F32)
    zero = jnp.zeros((n, lane_off), F32)
    padw = period - lane_off - 2 * half
    pad1, pad0 = jnp.ones((n, padw), F32), jnp.zeros((n, padw), F32)
    z = jnp.zeros_like(sin)
    cos_t = jnp.concatenate([one, cos, cos, pad1], axis=1)
    sin_a = jnp.concatenate([zero, -sin, z, pad0], axis=1)
    sin_b = jnp.concatenate([zero, z, sin, pad0], axis=1)
    rep = LANES // period
    return tuple(jnp.tile(t, (1, rep)) for t in (cos_t, sin_a, sin_b))


def _mla_weights(w_in, g_q, g_kv, w_uq, w_uk, w_uv):
    kr_cols = w_in[:, MLA_Q_RANK + MLA_KV_RANK:]
    slab = jnp.concatenate([jnp.zeros((D_MODEL, MLA_NOPE), F32), kr_cols,
                            jnp.zeros((D_MODEL, LANES - MLA_QK), F32)], axis=1)
    w_in_p = jnp.concatenate([w_in[:, :MLA_Q_RANK + MLA_KV_RANK], slab], axis=1)
    uq = w_uq.reshape(MLA_Q_RANK, N_HEADS, MLA_QK)
    uq = jnp.pad(uq, ((0, 0), (0, 0), (0, LANES - MLA_QK))).reshape(MLA_Q_RANK, N_HEADS * LANES)
    uk = jnp.pad(w_uk, ((0, 0), (0, 0), (0, LANES - MLA_NOPE))).reshape(MLA_KV_RANK, N_HEADS * LANES)
    return dict(
        w_in=w_in_p.astype(BF16), g_q=g_q.reshape(1, MLA_Q_RANK), g_kv=g_kv.reshape(1, MLA_KV_RANK),
        w_uq=uq.astype(BF16), w_uk=uk.astype(BF16),
        w_uv=w_uv.reshape(MLA_KV_RANK, N_HEADS * MLA_V).astype(BF16),
        w_ukt=jnp.transpose(w_uk, (1, 2, 0)).astype(BF16),
        w_uvh=jnp.transpose(w_uv, (1, 0, 2)).astype(BF16),
    )


def _repeat_q(q_hm, n_kv):
    q = jnp.transpose(q_hm, (1, 0, 2)).astype(F32)
    return jnp.tile(q, (1, 1, n_kv))


def kernel(x_prompt, x_sample, cache_sb_k, cache_sb_v, cache_mla_ckv, cache_mla_kr, cache_fox_k, cache_fox_v, cache_fox_logf, state_swa_k, state_swa_v, cache_mem_k, cache_mem_v, state_ffn_conv, page_table, mem_prompt, g_mix, w_sb_qkv, w_sb_o, w_mla_in, g_mla_q, g_mla_kv, w_mla_uq, w_mla_uk, w_mla_uv, w_mla_o, w_fox_in, b_fox_f, w_fox_o, w_swa_qkv, swa_sinks, w_swa_o, g_cross, g_mem, w_cq, w_ckv, w_co, g_ffn, w_ffn_in, w_ffn_conv, b_ffn_conv, w_ffn_out, g_final):
    bp, seq, _ = x_prompt.shape
    bs = x_sample.shape[0]
    n_pages = page_table.shape[1]
    past = n_pages * PAGE_SIZE
    depth = g_mix.shape[0]
    n_mem = mem_prompt.shape[1]
    mp = bp * seq
    win = state_swa_k.shape[2]
    pb = math.gcd(n_pages, 8)

    bf = lambda a: a.astype(BF16)
    pos_p = jnp.arange(seq)
    pos_s = jnp.full((bs,), past, jnp.int32)
    swa_tabs_p = _rope_tables(pos_p, HEAD_DIM // 2, HEAD_DIM, 0)
    swa_tabs_s = _rope_tables(pos_s, HEAD_DIM // 2, HEAD_DIM, 0)
    mla_tabs_p = _rope_tables(pos_p, MLA_ROPE // 2, LANES, MLA_NOPE)
    mla_tabs_s = _rope_tables(pos_s, MLA_ROPE // 2, LANES, MLA_NOPE)
    ident_pages = jnp.arange(bs, dtype=jnp.int32).reshape(bs, 1)

    outs = {}
    xp = x_prompt.reshape(mp, D_MODEL)
    xs = x_sample.reshape(bs, D_MODEL)
    mem_flat = mem_prompt.reshape(bp * n_mem, D_MODEL)
    mem_k_p, mem_v_p, conv_p, conv_s = [], [], [], []
    y_p = y_s = None

    for i in range(depth):
        m, j = i % 4, i // 4
        gm = g_mix[i]
        if m == 0:
            w = bf(w_sb_qkv[j])
            q, k, v, kh, vh = _proj_qkv(xp, gm, w)
            o_p = _attn_sb(q, kh, vh, bp, seq)
            outs.setdefault("sb_k_p", []).append(k.reshape(bp, seq, N_KV, HEAD_DIM))
            outs.setdefault("sb_v_p", []).append(v.reshape(bp, seq, N_KV, HEAD_DIM))
            q, k, v, kh, vh = _proj_qkv(xs, gm, w)
            o_s = _decode("sb", _repeat_q(q, N_KV),
                          cache_sb_k[j].reshape(-1, PAGE_SIZE, KV_W), cache_sb_v[j].reshape(-1, PAGE_SIZE, KV_W),
                          page_table, n_kv=N_KV, gq=GROUP, scale=1.0, pb=pb)
            o_s = o_s.reshape(bs, Q_W)
            outs.setdefault("sb_k_s", []).append(k.reshape(bs, 1, N_KV, HEAD_DIM))
            outs.setdefault("sb_v_s", []).append(v.reshape(bs, 1, N_KV, HEAD_DIM))
            w_o = bf(w_sb_o[j])
        elif m == 1:
            wp = _mla_weights(w_mla_in[j], g_mla_q[j], g_mla_kv[j], w_mla_uq[j], w_mla_uk[j], w_mla_uv[j])
            q, kh, v, ckv, kr = _proj_mla(xp, gm, wp, mla_tabs_p, absorbed=False)
            o_p = _attn("mla", q, kh, v, bp, seq)
            outs.setdefault("mla_ckv_p", []).append(ckv.reshape(bp, seq, MLA_KV_RANK))
            outs.setdefault("mla_kr_p", []).append(kr[:, MLA_NOPE:MLA_QK].reshape(bp, seq, MLA_ROPE))
            q, kh, v, ckv, kr, qlat = _proj_mla(xs, gm, wp, mla_tabs_s, absorbed=True)
            kr32 = kr[:, MLA_NOPE:MLA_QK]
            qr = jnp.transpose(q[:, :, MLA_NOPE:MLA_QK], (1, 0, 2)).astype(F32)
            o_lat = _decode("mla", jnp.transpose(qlat, (1, 0, 2)), cache_mla_ckv[j], None, page_table,
                            n_kv=1, gq=N_HEADS, scale=1.0, pb=pb, qr=qr, kr_pool=cache_mla_kr[j],
                            k_new=ckv.reshape(bs, 1, MLA_KV_RANK), kr_new=kr32.reshape(bs, 1, MLA_ROPE))
            o_s = _heads_matmul(jnp.transpose(o_lat, (1, 0, 2)), wp["w_uvh"])
            o_s = jnp.transpose(o_s, (1, 0, 2)).reshape(bs, Q_W)
            outs.setdefault("mla_ckv_s", []).append(ckv.reshape(bs, 1, MLA_KV_RANK))
            outs.setdefault("mla_kr_s", []).append(kr32.reshape(bs, 1, MLA_ROPE))
            w_o = bf(w_mla_o[j])
        elif m == 2:
            w = bf(w_fox_in[j])
            q, k, v, kh, vh, lf, c = _proj_qkv(xp, gm, w, b_f=b_fox_f[j], seq_len=seq)
            cq = jnp.transpose(c.reshape(mp, N_KV, GROUP), (1, 0, 2))
            ck = jnp.transpose(c.reshape(mp, N_KV, GROUP), (1, 2, 0))
            o_p = _attn("fox", q, kh, vh, bp, seq, cq=cq, ck=ck)
            outs.setdefault("fox_k_p", []).append(k.reshape(bp, seq, N_KV, HEAD_DIM))
            outs.setdefault("fox_v_p", []).append(v.reshape(bp, seq, N_KV, HEAD_DIM))
            outs.setdefault("fox_lf_p", []).append(lf.reshape(bp, seq, N_HEADS))
            q, k, v, kh, vh, lf, c = _proj_qkv(xs, gm, w, b_f=b_fox_f[j], seq_len=1)
            o_s = _decode("fox", _repeat_q(q, N_KV),
                          cache_fox_k[j].reshape(-1, PAGE_SIZE, KV_W), cache_fox_v[j].reshape(-1, PAGE_SIZE, KV_W),
                          page_table, n_kv=N_KV, gq=GROUP, scale=1.0, pb=pb,
                          lf_pool=jnp.transpose(cache_fox_logf[j], (0, 2, 1)),
                          k_new=k.reshape(bs, 1, KV_W), v_new=v.reshape(bs, 1, KV_W),
                          lf_new=lf.reshape(bs, N_HEADS, 1))
            o_s = o_s.reshape(bs, Q_W)
            outs.setdefault("fox_k_s", []).append(k.reshape(bs, 1, N_KV, HEAD_DIM))
            outs.setdefault("fox_v_s", []).append(v.reshape(bs, 1, N_KV, HEAD_DIM))
            outs.setdefault("fox_lf_s", []).append(lf.reshape(bs, 1, N_HEADS))
            w_o = bf(w_fox_o[j])
        else:
            w = bf(w_swa_qkv[j])
            q, k, v, kh, vh = _proj_qkv(xp, gm, w, rope_tabs=swa_tabs_p)
            o_p = _attn("swa", q, kh, vh, bp, seq, sinks=swa_sinks[j])
            outs.setdefault("swa_k_p", []).append(k.reshape(bp, seq, N_KV, HEAD_DIM)[:, seq - win:])
            outs.setdefault("swa_v_p", []).append(v.reshape(bp, seq, N_KV, HEAD_DIM)[:, seq - win:])
            q, k, v, kh, vh = _proj_qkv(xs, gm, w, rope_tabs=swa_tabs_s)
            o_s = _decode("swa", _repeat_q(q, N_KV),
                          state_swa_k[j].reshape(bs, win, KV_W), state_swa_v[j].reshape(bs, win, KV_W),
                          ident_pages, n_kv=N_KV, gq=GROUP, scale=1.0, pb=1,
                          k_new=k.reshape(bs, 1, KV_W), v_new=v.reshape(bs, 1, KV_W), sinks=swa_sinks[j])
            o_s = o_s.reshape(bs, Q_W)
            k_all = jnp.concatenate([state_swa_k[j], k.reshape(bs, 1, N_KV, HEAD_DIM)], axis=1)
            v_all = jnp.concatenate([state_swa_v[j], v.reshape(bs, 1, N_KV, HEAD_DIM)], axis=1)
            outs.setdefault("swa_k_s", []).append(k_all[:, -win:])
            outs.setdefault("swa_v_s", []).append(v_all[:, -win:])
            w_o = bf(w_swa_o[j])

        kv_f32, kv_b16 = _proj_plain(mem_flat, g_mem[i], bf(w_ckv[i]))
        mem_k_p.append(kv_f32[:, :MEM_W].reshape(bp, n_mem, MEM_HEADS, MEM_HEAD_DIM))
        mem_v_p.append(kv_f32[:, MEM_W:].reshape(bp, n_mem, MEM_HEADS, MEM_HEAD_DIM))
        w_cq_b, w_co_b = bf(w_cq[i]), bf(w_co[i])
        x1_p, oc_p = _oproj(xp, o_p, w_o, g_cross[i], w_cq_b,
                            mem=(kv_b16[:, :MEM_W], kv_b16[:, MEM_W:]), seq_len=seq)
        x1_s, qc_s = _oproj(xs, o_s, w_o, g_cross[i], w_cq_b)
        qc_rep = jnp.tile(qc_s.reshape(bs, MEM_HEADS, MEM_HEAD_DIM), (1, 1, MEM_HEADS))
        qc_rep = jnp.pad(qc_rep, ((0, 0), (0, SUBLANES - MEM_HEADS), (0, 0)))
        oc_s = _decode("cross", qc_rep, cache_mem_k[i].reshape(bs, n_mem, MEM_W),
                       cache_mem_v[i].reshape(bs, n_mem, MEM_W), ident_pages,
                       n_kv=MEM_HEADS, gq=1, scale=MEM_HEAD_DIM ** -0.5, pb=1)
        oc_s = oc_s[:, :MEM_HEADS].reshape(bs, MEM_W)

        fin = g_final if i == depth - 1 else None
        w_in_b, w_out_b = bf(w_ffn_in[i]), bf(w_ffn_out[i])
        res = _ffn(x1_p, oc_p, w_co_b, g_ffn[i], w_in_b, w_ffn_conv[i], b_ffn_conv[i], w_out_b,
                   seq_len=seq, g_final=fin)
        xp, tails = res[0], res[1]
        if fin is not None:
            y_p = res[2]
        tps = tails.shape[0] // bp
        conv_p.append(tails.reshape(bp, tps, SUBLANES, D_FF)[:, -1, SUBLANES - (CONV_W - 1):])
        buf = state_ffn_conv[i]
        res = _ffn(x1_s, oc_s, w_co_b, g_ffn[i], w_in_b, w_ffn_conv[i], b_ffn_conv[i], w_out_b,
                   buf=(buf[:, 0], buf[:, 1]), g_final=fin)
        xs, gate_s = res[0], res[1]
        if fin is not None:
            y_s = res[2]
        conv_s.append(jnp.stack([buf[:, 1], gate_s], axis=1))

    stk = lambda name: jnp.stack(outs[name])
    return (y_p.reshape(bp, seq, D_MODEL), y_s.reshape(bs, 1, D_MODEL),
            stk("sb_k_p"), stk("sb_v_p"), stk("sb_k_s"), stk("sb_v_s"),
            stk("mla_ckv_p"), stk("mla_kr_p"), stk("mla_ckv_s"), stk("mla_kr_s"),
            stk("fox_k_p"), stk("fox_v_p"), stk("fox_lf_p"),
            stk("fox_k_s"), stk("fox_v_s"), stk("fox_lf_s"),
            stk("swa_k_p"), stk("swa_v_p"), stk("swa_k_s"), stk("swa_v_s"),
            jnp.stack(mem_k_p), jnp.stack(mem_v_p),
            jnp.stack(conv_p), jnp.stack(conv_s))
```

```python
import functools
import math

import jax
import jax.numpy as jnp
from jax import lax
from jax.experimental import pallas as pl
from jax.experimental.pallas import tpu as pltpu

F32 = jnp.float32
BF16 = jnp.bfloat16

D_MODEL = 1024
N_HEADS = 16
HEAD_DIM = 64
N_KV = 4
GROUP = N_HEADS // N_KV
Q_W = N_HEADS * HEAD_DIM
KV_W = N_KV * HEAD_DIM
QKV_W = Q_W + 2 * KV_W
MLA_Q_RANK = 768
MLA_KV_RANK = 256
MLA_NOPE = 64
MLA_ROPE = 32
MLA_V = 64
MLA_QK = MLA_NOPE + MLA_ROPE
WINDOW = 128
MEM_HEADS = 4
MEM_HEAD_DIM = 128
MEM_W = MEM_HEADS * MEM_HEAD_DIM
D_FF = 2816
CONV_W = 3
ROPE_THETA = 10000.0
EPS = 1e-6
PAGE_SIZE = 128

LANES = 128
SUBLANES = 8
NEG_BIG = -1e30
VMEM_LIMIT = 56 * 1024 * 1024
CUM_CHUNK = 256
LOG2E = math.log2(math.e)
V_ONE = HEAD_DIM


def _cparams(*sem):
    return pltpu.CompilerParams(dimension_semantics=sem, vmem_limit_bytes=VMEM_LIMIT)


def _full(shape):
    return pl.BlockSpec(shape, lambda *_: (0,) * len(shape))


def _row_tile(m, pref):
    t = min(m, pref)
    assert m % t == 0, (m, t)
    return t


def _rms(x, g):
    return x * lax.rsqrt(jnp.mean(x * x, axis=-1, keepdims=True) + EPS) * g


def _dot(a, b):
    return jnp.dot(a, b, preferred_element_type=F32)


def _dot_nt(a, b):
    return lax.dot_general(a, b, (((1,), (1,)), ((), ())), preferred_element_type=F32)


def _log_sigmoid(z):
    return jnp.minimum(z, 0.0) - jnp.log1p(jnp.exp(-jnp.abs(z)))


def _log2_one_minus_sigmoid(z2):
    nz = -z2
    return jnp.minimum(nz, 0.0) - jnp.log2(1.0 + jnp.exp2(jnp.minimum(z2, nz)))


def _rope_slab(x, cos_t, sin_a, sin_b, half):
    return (x * cos_t + pltpu.roll(x, LANES - half, axis=1) * sin_a
            + pltpu.roll(x, half, axis=1) * sin_b)


def _rope_wide(x, cos_t, sin_a, sin_b, half):
    n = x.shape[1] // LANES
    return jnp.concatenate(
        [_rope_slab(x[:, i * LANES:(i + 1) * LANES], cos_t, sin_a, sin_b, half) for i in range(n)],
        axis=1)


def _split2(x):
    a = x.astype(BF16)
    b = (x - a.astype(F32)).astype(BF16)
    return a, b


def _split3(x):
    a = x.astype(BF16)
    r = x - a.astype(F32)
    b = r.astype(BF16)
    c = (r - b.astype(F32)).astype(BF16)
    return a, b, c


def _store_heads(dst_ref, a, n_heads, scale=None):
    for h in range(n_heads):
        blk = a[:, h * HEAD_DIM:(h + 1) * HEAD_DIM]
        if scale is not None:
            blk = blk * scale
        dst_ref[h] = blk.astype(dst_ref.dtype)


def _store_v_heads(dst_ref, v, n_heads):
    rows = v.shape[0]
    one_col = (lax.broadcasted_iota(jnp.int32, (rows, LANES - HEAD_DIM), 1) == 0).astype(F32)
    for h in range(n_heads):
        blk = jnp.concatenate([v[:, h * HEAD_DIM:(h + 1) * HEAD_DIM], one_col], axis=1)
        dst_ref[h] = blk.astype(dst_ref.dtype)


def _proj_qkv_body(*refs, rope, fox, nseq_tiles):
    it = iter(refs)
    x_ref, g_ref, w_ref = next(it), next(it), next(it)
    if rope:
        cos_ref, sa_ref, sb_ref = next(it), next(it), next(it)
    if fox:
        bf_ref, tri_ref = next(it), next(it)
    q_ref, k_ref, v_ref, kh_ref, vh_ref = next(it), next(it), next(it), next(it), next(it)
    if fox:
        lf_ref, c_ref, carry_sc = next(it), next(it), next(it)

    h = _rms(x_ref[...], g_ref[...]).astype(BF16)
    a = _dot(h, w_ref[...])
    q = a[:, :Q_W]
    k = a[:, Q_W:Q_W + KV_W]
    v = a[:, Q_W + KV_W:QKV_W]
    if rope:
        cos_t, sin_a, sin_b = cos_ref[...], sa_ref[...], sb_ref[...]
        q = _rope_wide(q, cos_t, sin_a, sin_b, HEAD_DIM // 2)
        k = _rope_wide(k, cos_t, sin_a, sin_b, HEAD_DIM // 2)
    k_ref[...] = k
    v_ref[...] = v
    _store_heads(q_ref, q, N_HEADS, scale=HEAD_DIM ** -0.5 * LOG2E)
    _store_heads(kh_ref, k, N_KV)
    _store_v_heads(vh_ref, v, N_KV)
    if fox:
        @pl.when(pl.program_id(0) % nseq_tiles == 0)
        def _():
            carry_sc[...] = jnp.zeros_like(carry_sc)

        lf = _log_sigmoid(a[:, QKV_W:QKV_W + N_HEADS] + bf_ref[...])
        lf_ref[...] = lf
        tri = tri_ref[...]
        t3 = _split3(lf)
        cs = _dot(tri, t3[0]) + _dot(tri, t3[1]) + _dot(tri, t3[2]) + carry_sc[...]
        c_ref[...] = cs * LOG2E
        carry_sc[...] = cs[cs.shape[0] - 1:, :]


def _proj_qkv(x, g, w, *, rope_tabs=None, b_f=None, seq_len=1):
    m = x.shape[0]
    rope = rope_tabs is not None
    fox = b_f is not None
    tm = _row_tile(min(m, seq_len) if fox and seq_len > 1 else m, 512)
    nseq_tiles = max(seq_len // tm, 1)
    in_specs = [pl.BlockSpec((tm, D_MODEL), lambda i: (i, 0)), _full((1, D_MODEL)), _full(w.shape)]
    args = [x, g.reshape(1, D_MODEL), w]
    if rope:
        nt = rope_tabs[0].shape[0] // tm
        for t in rope_tabs:
            in_specs.append(pl.BlockSpec((tm, LANES), lambda i, nt=nt: (i % nt, 0)))
            args.append(t)
    if fox:
        in_specs += [_full((1, N_HEADS)), _full((tm, tm))]
        args += [b_f.reshape(1, N_HEADS), jnp.tril(jnp.ones((tm, tm), F32)).astype(BF16)]
    out_shape = [
        jax.ShapeDtypeStruct((N_HEADS, m, HEAD_DIM), BF16),
        jax.ShapeDtypeStruct((m, KV_W), F32),
        jax.ShapeDtypeStruct((m, KV_W), F32),
        jax.ShapeDtypeStruct((N_KV, m, HEAD_DIM), BF16),
        jax.ShapeDtypeStruct((N_KV, m, LANES), BF16),
    ]
    out_specs = [
        pl.BlockSpec((N_HEADS, tm, HEAD_DIM), lambda i: (0, i, 0)),
        pl.BlockSpec((tm, KV_W), lambda i: (i, 0)),
        pl.BlockSpec((tm, KV_W), lambda i: (i, 0)),
        pl.BlockSpec((N_KV, tm, HEAD_DIM), lambda i: (0, i, 0)),
        pl.BlockSpec((N_KV, tm, LANES), lambda i: (0, i, 0)),
    ]
    scratch = []
    if fox:
        out_shape += [jax.ShapeDtypeStruct((m, N_HEADS), F32)] * 2
        out_specs += [pl.BlockSpec((tm, N_HEADS), lambda i: (i, 0))] * 2
        scratch = [pltpu.VMEM((1, N_HEADS), F32)]
    return pl.pallas_call(
        functools.partial(_proj_qkv_body, rope=rope, fox=fox, nseq_tiles=nseq_tiles),
        grid=(m // tm,), in_specs=in_specs, out_specs=out_specs, out_shape=out_shape,
        scratch_shapes=scratch, compiler_params=_cparams("arbitrary"),
        name="proj_fox" if fox else ("proj_swa" if rope else "proj_sb"),
    )(*args)


def _proj_mla_body(x_ref, g_ref, win_ref, gq_ref, gkv_ref, wuq_ref, wuk_ref, wuv_ref, wukt_ref,
                   cos_ref, sa_ref, sb_ref,
                   q_ref, kh_ref, vh_ref, ckv_ref, kr_ref, *maybe_qlat, absorbed):
    scale = MLA_QK ** -0.5 * LOG2E
    half = MLA_ROPE // 2
    cos_t, sin_a, sin_b = cos_ref[...], sa_ref[...], sb_ref[...]
    h = _rms(x_ref[...], g_ref[...]).astype(BF16)
    a = _dot(h, win_ref[...])
    cq = _rms(a[:, :MLA_Q_RANK], gq_ref[...]).astype(BF16)
    ckv = _rms(a[:, MLA_Q_RANK:MLA_Q_RANK + MLA_KV_RANK], gkv_ref[...])
    ckv_ref[...] = ckv
    ckv_b = ckv.astype(BF16)
    kr = _rope_slab(a[:, MLA_Q_RANK + MLA_KV_RANK:], cos_t, sin_a, sin_b, half)
    kr_ref[...] = kr
    qf = _dot(cq, wuq_ref[...])
    kn = _dot(ckv_b, wuk_ref[...])
    _store_v_heads(vh_ref, _dot(ckv_b, wuv_ref[...]), N_HEADS)
    for hd in range(N_HEADS):
        sl = slice(hd * LANES, (hd + 1) * LANES)
        qh = _rope_slab(qf[:, sl], cos_t, sin_a, sin_b, half) * scale
        q_ref[hd] = qh.astype(BF16)
        kh_ref[hd] = (kn[:, sl] + kr).astype(BF16)
        if absorbed:
            qn = qh[:, :MLA_NOPE].astype(BF16)
            maybe_qlat[0][hd] = _dot(qn, wukt_ref[hd])


def _proj_mla(x, g, wp, rope_tabs, *, absorbed):
    m = x.shape[0]
    tm = _row_tile(m, 512)
    nt = rope_tabs[0].shape[0] // tm
    args = [x, g.reshape(1, D_MODEL), wp["w_in"], wp["g_q"], wp["g_kv"], wp["w_uq"], wp["w_uk"],
            wp["w_uv"], wp["w_ukt"]]
    in_specs = [pl.BlockSpec((tm, D_MODEL), lambda i: (i, 0))] + [_full(a.shape) for a in args[1:]]
    for t in rope_tabs:
        in_specs.append(pl.BlockSpec((tm, LANES), lambda i, nt=nt: (i % nt, 0)))
        args.append(t)
    out_shape = [
        jax.ShapeDtypeStruct((N_HEADS, m, LANES), BF16),
        jax.ShapeDtypeStruct((N_HEADS, m, LANES), BF16),
        jax.ShapeDtypeStruct((N_HEADS, m, LANES), BF16),
        jax.ShapeDtypeStruct((m, MLA_KV_RANK), F32),
        jax.ShapeDtypeStruct((m, LANES), F32),
    ]
    hm = pl.BlockSpec((N_HEADS, tm, LANES), lambda i: (0, i, 0))
    out_specs = [hm, hm, hm,
                 pl.BlockSpec((tm, MLA_KV_RANK), lambda i: (i, 0)),
                 pl.BlockSpec((tm, LANES), lambda i: (i, 0))]
    if absorbed:
        out_shape.append(jax.ShapeDtypeStruct((N_HEADS, m, MLA_KV_RANK), F32))
        out_specs.append(pl.BlockSpec((N_HEADS, tm, MLA_KV_RANK), lambda i: (0, i, 0)))
    return pl.pallas_call(
        functools.partial(_proj_mla_body, absorbed=absorbed),
        grid=(m // tm,), in_specs=in_specs, out_specs=out_specs, out_shape=out_shape,
        compiler_params=_cparams("arbitrary"), name="proj_mla",
    )(*args)


def _proj_plain_body(x_ref, g_ref, w_ref, o_ref, ob_ref):
    a = _dot(_rms(x_ref[...], g_ref[...]).astype(BF16), w_ref[...])
    o_ref[...] = a
    ob_ref[...] = a.astype(BF16)


def _proj_plain(x, g, w):
    m, n = x.shape[0], w.shape[1]
    tm = _row_tile(m, 512)
    return pl.pallas_call(
        _proj_plain_body, grid=(m // tm,),
        in_specs=[pl.BlockSpec((tm, D_MODEL), lambda i: (i, 0)), _full((1, D_MODEL)), _full(w.shape)],
        out_specs=[pl.BlockSpec((tm, n), lambda i: (i, 0))] * 2,
        out_shape=[jax.ShapeDtypeStruct((m, n), F32), jax.ShapeDtypeStruct((m, n), BF16)],
        compiler_params=_cparams("arbitrary"), name="proj_mem",
    )(x, g.reshape(1, D_MODEL), w)


FLAG_FIRST, FLAG_LAST, FLAG_MASK = 1, 2, 4


def _schedule(n_tiles, tile, *, window=None, descending=False, strict=False):
    qi, kj, fl = [], [], []
    for i in range(n_tiles):
        q_lo = i * tile
        lo = 0 if window is None else max(0, (q_lo - window) // tile)
        js = list(range(lo, i + 1))
        if descending:
            js = js[::-1]
        for n, j in enumerate(js):
            k_hi = (j + 1) * tile - 1
            need_mask = (k_hi >= q_lo) if strict else (k_hi > q_lo)
            if window is not None:
                need_mask = True
            f = (FLAG_FIRST if n == 0 else 0) | (FLAG_LAST if n == len(js) - 1 else 0)
            f |= FLAG_MASK if need_mask else 0
            qi.append(i), kj.append(j), fl.append(f)
    return (jnp.asarray(qi, jnp.int32), jnp.asarray(kj, jnp.int32), jnp.asarray(fl, jnp.int32))


def _attn_body(qi_ref, kj_ref, fl_ref, *refs, mode, nh, tq, tk):
    it = iter(refs)
    if mode == "swa":
        sink_ref = next(it)
    q_ref, k_ref, v_ref = next(it), next(it), next(it)
    if mode == "fox":
        cq_ref, ck_ref = next(it), next(it)
    o_ref, m_sc, acc_sc = next(it), next(it), next(it)
    if mode == "fox":
        cqr_sc = next(it)
    g = pl.program_id(1)
    s_id = pl.program_id(2)
    flag = fl_ref[s_id]
    q_lo = qi_ref[s_id] * tq
    k_lo = kj_ref[s_id] * tk
    own_kv = mode == "mla"

    @pl.when((flag & FLAG_FIRST) != 0)
    def _():
        if mode == "swa":
            one_col = (lax.broadcasted_iota(jnp.int32, (tq, LANES), 1) == V_ONE).astype(F32)
            for hh in range(nh):
                m_sc[hh] = jnp.full((tq, LANES), sink_ref[g * nh + hh] * LOG2E, F32)
                acc_sc[hh] = one_col
        else:
            m_sc[...] = jnp.full(m_sc.shape, NEG_BIG, F32)
            acc_sc[...] = jnp.zeros_like(acc_sc)
        if mode == "fox":
            for hh in range(nh):
                cqr_sc[hh] = jnp.broadcast_to(cq_ref[0][:, hh:hh + 1], (tq, LANES))

    def step(masked):
        if masked:
            qpos = q_lo + lax.broadcasted_iota(jnp.int32, (tq, tk), 0)
            kpos = k_lo + lax.broadcasted_iota(jnp.int32, (tq, tk), 1)
            vis = kpos <= qpos
            if mode == "swa":
                vis = vis & (qpos - kpos <= WINDOW)
        for hh in range(nh):
            k = k_ref[hh] if own_kv else k_ref[0]
            v = v_ref[hh] if own_kv else v_ref[0]
            s = _dot_nt(q_ref[hh], k)
            if mode == "fox":
                s = s - ck_ref[0][hh:hh + 1, :]
            if masked:
                s = jnp.where(vis, s, NEG_BIG)
            m_prev = m_sc[hh]
            smax = jnp.max(s, axis=-1, keepdims=True)
            if mode == "fox":
                cq = cqr_sc[hh]
                m_new = jnp.maximum(m_prev, smax + cq)
                shift = m_new - cq
            else:
                m_new = jnp.maximum(m_prev, smax)
                shift = m_new
            alpha = jnp.exp2(m_prev - m_new)
            p = jnp.exp2(s - jnp.concatenate([shift] * (tk // LANES), axis=1))
            acc_sc[hh] = alpha * acc_sc[hh] + _dot(p.astype(BF16), v)
            m_sc[hh] = m_new

    if mode == "swa":
        step(True)
    else:
        @pl.when((flag & FLAG_MASK) != 0)
        def _():
            step(True)

        @pl.when((flag & FLAG_MASK) == 0)
        def _():
            step(False)

    @pl.when((flag & FLAG_LAST) != 0)
    def _():
        for hh in range(nh):
            acc = acc_sc[hh]
            o = acc[:, :HEAD_DIM] / acc[:, V_ONE:V_ONE + 1]
            o_ref[:, hh * HEAD_DIM:(hh + 1) * HEAD_DIM] = o.astype(o_ref.dtype)


def _attn(mode, q_hm, k_hm, v_hm, batch, seq, *, cq=None, ck=None, sinks=None):
    nh = GROUP
    tq = tk = _row_tile(seq, 512)
    nq = seq // tq
    dk = q_hm.shape[-1]
    nkv = nh if mode == "mla" else 1
    qi, kj, fl = _schedule(nq, tq, window=WINDOW if mode == "swa" else None)
    prefetch = [qi, kj, fl]
    if mode == "swa":
        prefetch.append(sinks.astype(F32))

    def qmap(b, g, s, qi, kj, *_):
        return (g, b * nq + qi[s], 0)

    def kmap(b, g, s, qi, kj, *_):
        return (g, b * nq + kj[s], 0)

    in_specs = [pl.BlockSpec((nh, tq, dk), qmap), pl.BlockSpec((nkv, tk, dk), kmap),
                pl.BlockSpec((nkv, tk, LANES), kmap)]
    args = [q_hm, k_hm, v_hm]
    if mode == "fox":
        in_specs.append(pl.BlockSpec((1, tq, GROUP), qmap))
        in_specs.append(pl.BlockSpec((1, GROUP, tk), lambda b, g, s, qi, kj, *_: (g, 0, b * nq + kj[s])))
        args += [cq, ck]
    grid_spec = pltpu.PrefetchScalarGridSpec(
        num_scalar_prefetch=len(prefetch), grid=(batch, N_HEADS // nh, int(qi.shape[0])),
        in_specs=in_specs,
        out_specs=pl.BlockSpec((tq, nh * HEAD_DIM), lambda b, g, s, qi, kj, *_: (b * nq + qi[s], g)),
        scratch_shapes=[pltpu.VMEM((nh, tq, LANES), F32)] * (3 if mode == "fox" else 2))
    return pl.pallas_call(
        functools.partial(_attn_body, mode=mode, nh=nh, tq=tq, tk=tk),
        grid_spec=grid_spec, out_shape=jax.ShapeDtypeStruct((batch * seq, Q_W), BF16),
        compiler_params=_cparams("arbitrary", "arbitrary", "arbitrary"), name="attn_" + mode,
    )(*prefetch, *args)


def _attn_sb_body(qi_ref, kj_ref, fl_ref, q_ref, k_ref, v_ref, u_ref, o_ref, run_sc, acc_sc, *, nh, tq, tk):
    s_id = pl.program_id(2)
    flag = fl_ref[s_id]
    q_lo = qi_ref[s_id] * tq
    k_lo = kj_ref[s_id] * tk
    ck = min(CUM_CHUNK, tk)
    n_chunks = tk // ck

    @pl.when((flag & FLAG_FIRST) != 0)
    def _():
        run_sc[...] = jnp.zeros_like(run_sc)
        acc_sc[...] = jnp.zeros_like(acc_sc)

    def step(masked):
        u = u_ref[...]
        for r in reversed(range(n_chunks)):
            kc = k_ref[0, r * ck:(r + 1) * ck, :]
            vc = v_ref[0, r * ck:(r + 1) * ck, :]
            if masked:
                qpos = q_lo + lax.broadcasted_iota(jnp.int32, (tq, ck), 0)
                kpos = k_lo + r * ck + lax.broadcasted_iota(jnp.int32, (tq, ck), 1)
                vis = kpos < qpos
            for hh in range(nh):
                z = _dot_nt(q_ref[hh], kc)
                lr = _log2_one_minus_sigmoid(z)
                if masked:
                    lr = jnp.where(vis, lr, 0.0)
                incl = _dot(jnp.concatenate(_split2(lr), axis=1), u)
                run = run_sc[hh]
                w = jnp.exp2(z + incl + jnp.concatenate([run] * (ck // LANES), axis=1))
                if masked:
                    w = jnp.where(vis, w, 0.0)
                acc_sc[hh] = acc_sc[hh] + _dot(w.astype(BF16), vc)
                run_sc[hh] = run + jnp.broadcast_to(incl[:, 0:1], (tq, LANES))

    @pl.when((flag & FLAG_MASK) != 0)
    def _():
        step(True)

    @pl.when((flag & FLAG_MASK) == 0)
    def _():
        step(False)

    @pl.when((flag & FLAG_LAST) != 0)
    def _():
        for hh in range(nh):
            o_ref[:, hh * HEAD_DIM:(hh + 1) * HEAD_DIM] = acc_sc[hh][:, :HEAD_DIM].astype(o_ref.dtype)


def _attn_sb(q_hm, k_hm, v_hm, batch, seq):
    nh = GROUP
    tq = tk = _row_tile(seq, 512)
    nq = seq // tq
    ck = min(CUM_CHUNK, tk)
    qi, kj, fl = _schedule(nq, tq, descending=True, strict=True)
    u = (jnp.arange(ck)[:, None] >= jnp.arange(ck)[None, :]).astype(BF16)
    u = jnp.concatenate([u, u], axis=0)

    def qmap(b, g, s, qi, kj, fl):
        return (g, b * nq + qi[s], 0)

    def kmap(b, g, s, qi, kj, fl):
        return (g, b * nq + kj[s], 0)

    grid_spec = pltpu.PrefetchScalarGridSpec(
        num_scalar_prefetch=3, grid=(batch, N_KV, int(qi.shape[0])),
        in_specs=[pl.BlockSpec((nh, tq, HEAD_DIM), qmap), pl.BlockSpec((1, tk, HEAD_DIM), kmap),
                  pl.BlockSpec((1, tk, LANES), kmap), pl.BlockSpec((2 * ck, ck), lambda *_: (0, 0))],
        out_specs=pl.BlockSpec((tq, nh * HEAD_DIM), lambda b, g, s, qi, kj, fl: (b * nq + qi[s], g)),
        scratch_shapes=[pltpu.VMEM((nh, tq, LANES), F32), pltpu.VMEM((nh, tq, LANES), F32)])
    return pl.pallas_call(
        functools.partial(_attn_sb_body, nh=nh, tq=tq, tk=tk),
        grid_spec=grid_spec, out_shape=jax.ShapeDtypeStruct((batch * seq, Q_W), BF16),
        compiler_params=_cparams("arbitrary", "arbitrary", "arbitrary"), name="attn_sb",
    )(qi, kj, fl, q_hm, k_hm, v_hm, u)


def _suffix_sum_lanes(x):
    h, w = x.shape
    n = w // LANES
    lane = lax.broadcasted_iota(jnp.int32, (h, LANES), 1)
    slabs = []
    for i in range(n):
        y = x[:, i * LANES:(i + 1) * LANES]
        d = 1
        while d < LANES:
            y = y + jnp.where(lane < LANES - d, pltpu.roll(y, LANES - d, axis=1), 0.0)
            d *= 2
        slabs.append(y)
    tots = [jnp.broadcast_to(y[:, 0:1], (h, LANES)) for y in slabs]
    run = jnp.zeros((h, LANES), F32)
    out = [None] * n
    for i in reversed(range(n)):
        out[i] = slabs[i] + run
        run = run + tots[i]
    return (jnp.concatenate(out, axis=1) if n > 1 else out[0]), run


def _decode_body(pt_ref, *refs, mode, pb, hq, new_key):
    it = iter(refs)
    q_ref = next(it)
    qr_ref = next(it) if mode == "mla" else None
    k_refs = [next(it) for _ in range(pb)]
    kr_refs = [next(it) for _ in range(pb)] if mode == "mla" else None
    v_refs = k_refs if mode == "mla" else [next(it) for _ in range(pb)]
    lf_refs = [next(it) for _ in range(pb)] if mode == "fox" else None
    if new_key:
        kn_ref = next(it)
        krn_ref = next(it) if mode == "mla" else None
        vn_ref = kn_ref if mode == "mla" else next(it)
        lfn_ref = next(it) if mode == "fox" else None
    sink_ref = next(it) if mode == "swa" else None
    o_ref, m_sc, l_sc, acc_sc, car_sc = next(it), next(it), next(it), next(it), next(it)
    t = pl.program_id(1)
    dk = q_ref.shape[-1]
    grouped = mode in ("sb", "fox", "swa")
    softmax = mode != "sb"

    if grouped:
        row_kv = lax.broadcasted_iota(jnp.int32, (hq, dk), 0) // GROUP
        lane_kv = lax.broadcasted_iota(jnp.int32, (hq, dk), 1) // HEAD_DIM
        q = jnp.where(row_kv == lane_kv, q_ref[0], 0.0)
    else:
        q = q_ref[0]
    qb = q.astype(BF16)

    @pl.when(t == 0)
    def _():
        if mode == "swa":
            m_sc[...] = jnp.broadcast_to(sink_ref[...] * LOG2E, m_sc.shape)
            l_sc[...] = jnp.ones_like(l_sc)
        else:
            m_sc[...] = jnp.full(m_sc.shape, NEG_BIG, F32)
            l_sc[...] = jnp.zeros_like(l_sc)
        acc_sc[...] = jnp.zeros_like(acc_sc)
        if mode == "fox" and new_key:
            car_sc[...] = jnp.broadcast_to(lfn_ref[0] * LOG2E, car_sc.shape)
        else:
            car_sc[...] = jnp.zeros_like(car_sc)

    order = list(reversed(range(pb)))
    s_parts = []
    for r in order:
        kp = k_refs[r][0].astype(BF16)
        if grouped:
            s = _dot(qb, kp)
        else:
            s = _dot_nt(qb, kp)
        if mode == "mla":
            s = s + _dot(qr_ref[0].astype(BF16), kr_refs[r][0].astype(BF16))
        s_parts.append(s)
    s = jnp.concatenate(s_parts, axis=1) if pb > 1 else s_parts[0]
    ps = s.shape[1] // pb
    nrep = s.shape[1] // LANES
    if mode == "cross":
        col_head = lax.broadcasted_iota(jnp.int32, s.shape, 1) % MEM_HEADS
        row_head = lax.broadcasted_iota(jnp.int32, s.shape, 0)
        s = jnp.where(col_head == row_head, s, NEG_BIG)

    def wide(x):
        return jnp.concatenate([x] * nrep, axis=1) if nrep > 1 else x

    def pv(p):
        pb16 = p.astype(BF16)
        out = None
        for n, r in enumerate(order):
            blk = pb16[:, n * ps:(n + 1) * ps]
            vp = v_refs[r][0].astype(BF16)
            term = _dot_nt(blk, vp) if grouped else _dot(blk, vp)
            out = term if out is None else out + term
        return out

    if mode == "sb":
        lr = _log2_one_minus_sigmoid(s)
        incl, tot = _suffix_sum_lanes(lr)
        w = jnp.exp2(s + incl + wide(car_sc[...]))
        acc_sc[...] = acc_sc[...] + pv(w)
        car_sc[...] = car_sc[...] + tot
    else:
        if mode == "fox":
            lf = jnp.concatenate([lf_refs[r][0] for r in order], axis=1) if pb > 1 else lf_refs[0][0]
            lf = lf * LOG2E
            incl, tot = _suffix_sum_lanes(lf)
            s = s + (incl - lf) + wide(car_sc[...])
            car_sc[...] = car_sc[...] + tot
        m_prev = m_sc[...]
        m_new = jnp.maximum(m_prev, jnp.max(s, axis=-1, keepdims=True))
        alpha = jnp.exp2(m_prev - m_new)
        p = jnp.exp2(s - wide(m_new))
        l_sc[...] = alpha * l_sc[...] + jnp.sum(p, axis=-1, keepdims=True)
        nacc = acc_sc.shape[-1] // LANES
        alpha_w = jnp.concatenate([alpha] * nacc, axis=1) if nacc > 1 else alpha
        acc_sc[...] = alpha_w * acc_sc[...] + pv(p)
        m_sc[...] = m_new

    @pl.when(t == pl.num_programs(1) - 1)
    def _():
        acc = acc_sc[...]
        if softmax:
            m_prev, l_prev = m_sc[...][:, :1], l_sc[...][:, :1]
            if new_key:
                s_new = jnp.sum(q * kn_ref[0], axis=-1, keepdims=True)
                if mode == "mla":
                    s_new = s_new + jnp.sum(qr_ref[0] * krn_ref[0], axis=-1, keepdims=True)
                m_new = jnp.maximum(m_prev, s_new)
                alpha = jnp.exp2(m_prev - m_new)
                p_new = jnp.exp2(s_new - m_new)
                l_prev = alpha * l_prev + p_new
                acc = alpha * acc + p_new * vn_ref[0]
            acc = acc / l_prev
        if grouped:
            row_kv = lax.broadcasted_iota(jnp.int32, (hq, HEAD_DIM), 0) // GROUP
            out = jnp.zeros((hq, HEAD_DIM), F32)
            for kv in range(N_KV):
                out = out + jnp.where(row_kv == kv, acc[:, kv * HEAD_DIM:(kv + 1) * HEAD_DIM], 0.0)
            o_ref[0] = out
        else:
            o_ref[0] = acc


def _decode(mode, q, k_pool, v_pool, page_table, *, pb, qr=None, kr_pool=None,
            lf_pool=None, k_new=None, v_new=None, kr_new=None, lf_new=None, sinks=None):
    bsz, hq, dk = q.shape
    n_pages = page_table.shape[1]
    assert n_pages % pb == 0
    grouped = mode in ("sb", "fox", "swa")
    kblk = k_pool.shape[1:]
    vblk = (v_pool if v_pool is not None else k_pool).shape[1:]
    dvt = vblk[0] if grouped else vblk[1]
    out_d = HEAD_DIM if grouped else dvt
    new_key = k_new is not None
    pt = page_table.reshape(-1).astype(jnp.int32)

    def page_map(r):
        return lambda b, t, pt: (pt[b * n_pages + n_pages - 1 - (t * pb + r)], 0, 0)

    def seq_map(b, t, pt):
        return (b, 0, 0)

    in_specs = [pl.BlockSpec((1, hq, dk), seq_map)]
    args = [q]
    if mode == "mla":
        in_specs.append(pl.BlockSpec((1, hq, qr.shape[-1]), seq_map))
        args.append(qr)
    for r in range(pb):
        in_specs.append(pl.BlockSpec((1,) + kblk, page_map(r)))
        args.append(k_pool)
    if mode == "mla":
        for r in range(pb):
            in_specs.append(pl.BlockSpec((1,) + kr_pool.shape[1:], page_map(r)))
            args.append(kr_pool)
    else:
        for r in range(pb):
            in_specs.append(pl.BlockSpec((1,) + vblk, page_map(r)))
            args.append(v_pool)
    if mode == "fox":
        for r in range(pb):
            in_specs.append(pl.BlockSpec((1,) + lf_pool.shape[1:], page_map(r)))
            args.append(lf_pool)
    if new_key:
        in_specs.append(pl.BlockSpec((1, 1, dk), seq_map))
        args.append(k_new)
        if mode == "mla":
            in_specs.append(pl.BlockSpec((1, 1, kr_new.shape[-1]), seq_map))
            args.append(kr_new)
        else:
            in_specs.append(pl.BlockSpec((1, 1, dvt), seq_map))
            args.append(v_new)
        if mode == "fox":
            in_specs.append(pl.BlockSpec((1, hq, 1), seq_map))
            args.append(lf_new)
    if mode == "swa":
        in_specs.append(pl.BlockSpec((hq, 1), lambda b, t, pt: (0, 0)))
        args.append(sinks.reshape(hq, 1).astype(F32))
    grid_spec = pltpu.PrefetchScalarGridSpec(
        num_scalar_prefetch=1, grid=(bsz, n_pages // pb), in_specs=in_specs,
        out_specs=pl.BlockSpec((1, hq, out_d), seq_map),
        scratch_shapes=[pltpu.VMEM((hq, LANES), F32), pltpu.VMEM((hq, LANES), F32),
                        pltpu.VMEM((hq, dvt), F32), pltpu.VMEM((hq, LANES), F32)])
    return pl.pallas_call(
        functools.partial(_decode_body, mode=mode, pb=pb, hq=hq, new_key=new_key),
        grid_spec=grid_spec, out_shape=jax.ShapeDtypeStruct((bsz, hq, out_d), F32),
        compiler_params=_cparams("arbitrary", "arbitrary"), name="decode_" + mode,
    )(pt, *args)


def _heads_matmul_body(x_ref, w_ref, o_ref):
    o_ref[0] = _dot(x_ref[0].astype(BF16), w_ref[0])


def _heads_matmul(x, w):
    h, m, k = x.shape
    n = w.shape[-1]
    return pl.pallas_call(
        _heads_matmul_body, grid=(h,),
        in_specs=[pl.BlockSpec((1, m, k), lambda i: (i, 0, 0)), pl.BlockSpec((1, k, n), lambda i: (i, 0, 0))],
        out_specs=pl.BlockSpec((1, m, n), lambda i: (i, 0, 0)),
        out_shape=jax.ShapeDtypeStruct((h, m, n), F32),
        compiler_params=_cparams("arbitrary"), name="mla_uv",
    )(x, w)


def _oproj_body(x_ref, o_ref, wo_ref, gc_ref, wcq_ref, *refs, cross):
    if cross:
        mk_ref, mv_ref, x1_ref, oc_ref = refs
    else:
        x1_ref, qc_ref = refs
    x1 = x_ref[...] + _dot(o_ref[...].astype(BF16), wo_ref[...])
    x1_ref[...] = x1
    qc = _dot(_rms(x1, gc_ref[...]).astype(BF16), wcq_ref[...]) * (MEM_HEAD_DIM ** -0.5 * LOG2E)
    if not cross:
        qc_ref[...] = qc
        return
    qc = qc.astype(BF16)
    for hh in range(MEM_HEADS):
        sl = slice(hh * MEM_HEAD_DIM, (hh + 1) * MEM_HEAD_DIM)
        s = _dot_nt(qc[:, sl], mk_ref[:, sl])
        p = jnp.exp2(s - jnp.max(s, axis=-1, keepdims=True))
        o = _dot(p.astype(BF16), mv_ref[:, sl]) / jnp.sum(p, axis=-1, keepdims=True)
        oc_ref[:, sl] = o.astype(oc_ref.dtype)


def _oproj(x, o, w_o, g_c, w_cq, *, mem=None, seq_len=1):
    m = x.shape[0]
    tm = _row_tile(min(m, seq_len) if mem is not None else m, 512)
    cross = mem is not None
    in_specs = [pl.BlockSpec((tm, D_MODEL), lambda i: (i, 0)), pl.BlockSpec((tm, o.shape[1]), lambda i: (i, 0)),
                _full(w_o.shape), _full((1, D_MODEL)), _full(w_cq.shape)]
    args = [x, o, w_o, g_c.reshape(1, D_MODEL), w_cq]
    if cross:
        n_mem = mem[0].shape[0] // (m // seq_len)
        tiles_per_seq = seq_len // tm
        for a in mem:
            in_specs.append(pl.BlockSpec((n_mem, MEM_W), lambda i: (i // tiles_per_seq, 0)))
            args.append(a)
    out_shape = [jax.ShapeDtypeStruct((m, D_MODEL), F32),
                 jax.ShapeDtypeStruct((m, MEM_W), BF16 if cross else F32)]
    out_specs = [pl.BlockSpec((tm, D_MODEL), lambda i: (i, 0)), pl.BlockSpec((tm, MEM_W), lambda i: (i, 0))]
    return pl.pallas_call(
        functools.partial(_oproj_body, cross=cross), grid=(m // tm,),
        in_specs=in_specs, out_specs=out_specs, out_shape=out_shape,
        compiler_params=_cparams("arbitrary"), name="oproj_cross" if cross else "oproj",
    )(*args)


def _ffn_body(*refs, seq_mode, final, nseq_tiles, tm, fc):
    it = iter(refs)
    x1_ref, oc_ref, wco_ref, gf_ref = next(it), next(it), next(it), next(it)
    wg_ref, wv_ref, wc_ref, bc_ref, wout_ref = next(it), next(it), next(it), next(it), next(it)
    if not seq_mode:
        b0_ref, b1_ref = next(it), next(it)
    if final:
        gfin_ref = next(it)
    x3_ref, gate_ref = next(it), next(it)
    if final:
        y_ref = next(it)
    x2_sc, h_sc, acc_sc = next(it), next(it), next(it)
    if seq_mode:
        halo_sc, gs_sc = next(it), next(it)
    i = pl.program_id(0)
    f = pl.program_id(1)

    @pl.when(f == 0)
    def _():
        x2 = x1_ref[...] + _dot(oc_ref[...].astype(BF16), wco_ref[...])
        x2_sc[...] = x2
        h_sc[...] = _rms(x2, gf_ref[...]).astype(BF16)
        acc_sc[...] = jnp.zeros_like(acc_sc)

    h = h_sc[...]
    gate = _dot(h, wg_ref[...])
    val = _dot(h, wv_ref[...])
    wc = wc_ref[...]
    if seq_mode:
        @pl.when(i % nseq_tiles == 0)
        def _():
            gs_sc[0:SUBLANES, :] = jnp.zeros((SUBLANES, fc), F32)

        @pl.when(i % nseq_tiles != 0)
        def _():
            gs_sc[0:SUBLANES, :] = halo_sc[f]

        gs_sc[SUBLANES:, :] = gate
        gm1 = gs_sc[SUBLANES - 1:SUBLANES - 1 + tm, :]
        gm2 = gs_sc[SUBLANES - 2:SUBLANES - 2 + tm, :]
        tail = gate[tm - SUBLANES:, :]
        halo_sc[f] = tail
        gate_ref[0] = tail
    else:
        gm2, gm1 = b0_ref[...], b1_ref[...]
        gate_ref[...] = gate
    conv = bc_ref[...] + wc[0:1, :] * gm2 + wc[1:2, :] * gm1 + wc[2:3, :] * gate
    y = 0.5 * conv * (1.0 + lax.erf(conv * (2.0 ** -0.5))) * val
    acc_sc[...] = acc_sc[...] + _dot(y.astype(BF16), wout_ref[...])

    @pl.when(f == pl.num_programs(1) - 1)
    def _():
        x3 = x2_sc[...] + acc_sc[...]
        x3_ref[...] = x3
        if final:
            y_ref[...] = _rms(x3, gfin_ref[...])


def _ffn(x1, oc, w_co, g_f, w_in, w_conv, b_conv, w_out, *, seq_len=1, buf=None, g_final=None):
    m = x1.shape[0]
    seq_mode = buf is None
    final = g_final is not None
    tm = _row_tile(min(m, seq_len) if seq_mode else m, 512)
    nseq_tiles = max(seq_len // tm, 1)
    fc = D_FF // 2
    nf = D_FF // fc
    row = lambda i, f: (i, 0)
    in_specs = [pl.BlockSpec((tm, D_MODEL), row), pl.BlockSpec((tm, MEM_W), row), _full(w_co.shape),
                _full((1, D_MODEL)),
                pl.BlockSpec((D_MODEL, fc), lambda i, f: (0, f)),
                pl.BlockSpec((D_MODEL, fc), lambda i, f: (0, nf + f)),
                pl.BlockSpec((CONV_W, fc), lambda i, f: (0, f)),
                pl.BlockSpec((1, fc), lambda i, f: (0, f)),
                pl.BlockSpec((fc, D_MODEL), lambda i, f: (f, 0))]
    args = [x1, oc, w_co, g_f.reshape(1, D_MODEL), w_in, w_in, w_conv, b_conv.reshape(1, D_FF), w_out]
    if not seq_mode:
        in_specs += [pl.BlockSpec((tm, fc), lambda i, f: (i, f))] * 2
        args += [buf[0], buf[1]]
    if final:
        in_specs.append(_full((1, D_MODEL)))
        args.append(g_final.reshape(1, D_MODEL))
    out_shape = [jax.ShapeDtypeStruct((m, D_MODEL), F32)]
    out_specs = [pl.BlockSpec((tm, D_MODEL), row)]
    if seq_mode:
        out_shape.append(jax.ShapeDtypeStruct((m // tm, SUBLANES, D_FF), F32))
        out_specs.append(pl.BlockSpec((1, SUBLANES, fc), lambda i, f: (i, 0, f)))
    else:
        out_shape.append(jax.ShapeDtypeStruct((m, D_FF), F32))
        out_specs.append(pl.BlockSpec((tm, fc), lambda i, f: (i, f)))
    if final:
        out_shape.append(jax.ShapeDtypeStruct((m, D_MODEL), F32))
        out_specs.append(pl.BlockSpec((tm, D_MODEL), row))
    scratch = [pltpu.VMEM((tm, D_MODEL), F32), pltpu.VMEM((tm, D_MODEL), BF16), pltpu.VMEM((tm, D_MODEL), F32)]
    if seq_mode:
        scratch += [pltpu.VMEM((nf, SUBLANES, fc), F32), pltpu.VMEM((tm + SUBLANES, fc), F32)]
    return pl.pallas_call(
        functools.partial(_ffn_body, seq_mode=seq_mode, final=final, nseq_tiles=nseq_tiles, tm=tm, fc=fc),
        grid=(m // tm, nf), in_specs=in_specs, out_specs=out_specs, out_shape=out_shape,
        scratch_shapes=scratch, compiler_params=_cparams("arbitrary", "arbitrary"),
        name="ffn_seq" if seq_mode else "ffn_tok",
    )(*args)


def _rope_tables(pos, half, period, lane_off):
    inv = jnp.exp(-math.log(ROPE_THETA) * jnp.arange(half, dtype=F32) / half)
    ang = pos.astype(F32)[:, None] * inv[None, :]
    cos, sin = jnp.cos(ang), jnp.sin(ang)
    n = pos.shape[0]
    one = jnp.ones((n, lane_off), F32)
    zero = jnp.zeros((n, lane_off), F32)
    padw = period - lane_off - 2 * half
    pad1, pad0 = jnp.ones((n, padw), F32), jnp.zeros((n, padw), F32)
    z = jnp.zeros_like(sin)
    cos_t = jnp.concatenate([one, cos, cos, pad1], axis=1)
    sin_a = jnp.concatenate([zero, -sin, z, pad0], axis=1)
    sin_b = jnp.concatenate([zero, z, sin, pad0], axis=1)
    rep = LANES // period
    return tuple(jnp.tile(t, (1, rep)) for t in (cos_t, sin_a, sin_b))


def _mla_weights(w_in, g_q, g_kv, w_uq, w_uk, w_uv):
    kr_cols = w_in[:, MLA_Q_RANK + MLA_KV_RANK:]
    slab = jnp.concatenate([jnp.zeros((D_MODEL, MLA_NOPE), F32), kr_cols,
                            jnp.zeros((D_MODEL, LANES - MLA_QK), F32)], axis=1)
    w_in_p = jnp.concatenate([w_in[:, :MLA_Q_RANK + MLA_KV_RANK], slab], axis=1)
    uq = w_uq.reshape(MLA_Q_RANK, N_HEADS, MLA_QK)
    uq = jnp.pad(uq, ((0, 0), (0, 0), (0, LANES - MLA_QK))).reshape(MLA_Q_RANK, N_HEADS * LANES)
    uk = jnp.pad(w_uk, ((0, 0), (0, 0), (0, LANES - MLA_NOPE))).reshape(MLA_KV_RANK, N_HEADS * LANES)
    return dict(
        w_in=w_in_p.astype(BF16), g_q=g_q.reshape(1, MLA_Q_RANK), g_kv=g_kv.reshape(1, MLA_KV_RANK),
        w_uq=uq.astype(BF16), w_uk=uk.astype(BF16),
        w_uv=w_uv.reshape(MLA_KV_RANK, N_HEADS * MLA_V).astype(BF16),
        w_ukt=jnp.transpose(w_uk, (1, 2, 0)).astype(BF16),
        w_uvh=jnp.transpose(w_uv, (1, 0, 2)).astype(BF16),
    )


def _repeat_q(q_hm, n_rep):
    q = jnp.transpose(q_hm, (1, 0, 2)).astype(F32)
    return jnp.tile(q, (1, 1, n_rep))


def _kv_pages_t(cache):
    p, s = cache.shape[0], cache.shape[1]
    return jnp.transpose(cache, (0, 2, 3, 1)).reshape(p, KV_W, s)


def kernel(x_prompt, x_sample, cache_sb_k, cache_sb_v, cache_mla_ckv, cache_mla_kr, cache_fox_k, cache_fox_v, cache_fox_logf, state_swa_k, state_swa_v, cache_mem_k, cache_mem_v, state_ffn_conv, page_table, mem_prompt, g_mix, w_sb_qkv, w_sb_o, w_mla_in, g_mla_q, g_mla_kv, w_mla_uq, w_mla_uk, w_mla_uv, w_mla_o, w_fox_in, b_fox_f, w_fox_o, w_swa_qkv, swa_sinks, w_swa_o, g_cross, g_mem, w_cq, w_ckv, w_co, g_ffn, w_ffn_in, w_ffn_conv, b_ffn_conv, w_ffn_out, g_final):
    bp, seq, _ = x_prompt.shape
    bs = x_sample.shape[0]
    n_pages = page_table.shape[1]
    past = n_pages * PAGE_SIZE
    depth = g_mix.shape[0]
    n_mem = mem_prompt.shape[1]
    mp = bp * seq
    win = state_swa_k.shape[2]
    pb = math.gcd(n_pages, 32)

    bf = lambda a: a.astype(BF16)
    pos_p = jnp.arange(seq)
    pos_s = jnp.full((bs,), past, jnp.int32)
    swa_tabs_p = _rope_tables(pos_p, HEAD_DIM // 2, HEAD_DIM, 0)
    swa_tabs_s = _rope_tables(pos_s, HEAD_DIM // 2, HEAD_DIM, 0)
    mla_tabs_p = _rope_tables(pos_p, MLA_ROPE // 2, LANES, MLA_NOPE)
    mla_tabs_s = _rope_tables(pos_s, MLA_ROPE // 2, LANES, MLA_NOPE)
    ident_pages = jnp.arange(bs, dtype=jnp.int32).reshape(bs, 1)
    mem_k_rows = cache_mem_k.reshape(depth * bs, n_mem * MEM_HEADS, MEM_HEAD_DIM)
    mem_v_rows = cache_mem_v.reshape(depth * bs, n_mem * MEM_HEADS, MEM_HEAD_DIM)

    outs = {}
    xp = x_prompt.reshape(mp, D_MODEL)
    xs = x_sample.reshape(bs, D_MODEL)
    mem_flat = mem_prompt.reshape(bp * n_mem, D_MODEL)
    mem_k_p, mem_v_p, conv_p, conv_s = [], [], [], []
    y_p = y_s = None

    for i in range(depth):
        m, j = i % 4, i // 4
        gm = g_mix[i]
        if m == 0:
            w = bf(w_sb_qkv[j])
            q, k, v, kh, vh = _proj_qkv(xp, gm, w)
            o_p = _attn_sb(q, kh, vh, bp, seq)
            outs.setdefault("sb_k_p", []).append(k.reshape(bp, seq, N_KV, HEAD_DIM))
            outs.setdefault("sb_v_p", []).append(v.reshape(bp, seq, N_KV, HEAD_DIM))
            q, k, v, kh, vh = _proj_qkv(xs, gm, w)
            o_s = _decode("sb", _repeat_q(q, N_KV), _kv_pages_t(cache_sb_k[j]), _kv_pages_t(cache_sb_v[j]),
                          page_table, pb=pb)
            o_s = o_s.reshape(bs, Q_W)
            outs.setdefault("sb_k_s", []).append(k.reshape(bs, 1, N_KV, HEAD_DIM))
            outs.setdefault("sb_v_s", []).append(v.reshape(bs, 1, N_KV, HEAD_DIM))
            w_o = bf(w_sb_o[j])
        elif m == 1:
            wp = _mla_weights(w_mla_in[j], g_mla_q[j], g_mla_kv[j], w_mla_uq[j], w_mla_uk[j], w_mla_uv[j])
            q, kh, vh, ckv, kr = _proj_mla(xp, gm, wp, mla_tabs_p, absorbed=False)
            o_p = _attn("mla", q, kh, vh, bp, seq)
            outs.setdefault("mla_ckv_p", []).append(ckv.reshape(bp, seq, MLA_KV_RANK))
            outs.setdefault("mla_kr_p", []).append(kr[:, MLA_NOPE:MLA_QK].reshape(bp, seq, MLA_ROPE))
            q, kh, vh, ckv, kr, qlat = _proj_mla(xs, gm, wp, mla_tabs_s, absorbed=True)
            kr32 = kr[:, MLA_NOPE:MLA_QK]
            qr = jnp.transpose(q[:, :, MLA_NOPE:MLA_QK], (1, 0, 2)).astype(F32)
            o_lat = _decode("mla", jnp.transpose(qlat, (1, 0, 2)), cache_mla_ckv[j], None, page_table,
                            pb=pb, qr=qr, kr_pool=jnp.transpose(cache_mla_kr[j], (0, 2, 1)),
                            k_new=ckv.reshape(bs, 1, MLA_KV_RANK), kr_new=kr32.reshape(bs, 1, MLA_ROPE))
            o_s = _heads_matmul(jnp.transpose(o_lat, (1, 0, 2)), wp["w_uvh"])
            o_s = jnp.transpose(o_s, (1, 0, 2)).reshape(bs, Q_W)
            outs.setdefault("mla_ckv_s", []).append(ckv.reshape(bs, 1, MLA_KV_RANK))
            outs.setdefault("mla_kr_s", []).append(kr32.reshape(bs, 1, MLA_ROPE))
            w_o = bf(w_mla_o[j])
        elif m == 2:
            w = bf(w_fox_in[j])
            q, k, v, kh, vh, lf, c = _proj_qkv(xp, gm, w, b_f=b_fox_f[j], seq_len=seq)
            cq = jnp.transpose(c.reshape(mp, N_KV, GROUP), (1, 0, 2))
            ck = jnp.transpose(c.reshape(mp, N_KV, GROUP), (1, 2, 0))
            o_p = _attn("fox", q, kh, vh, bp, seq, cq=cq, ck=ck)
            outs.setdefault("fox_k_p", []).append(k.reshape(bp, seq, N_KV, HEAD_DIM))
            outs.setdefault("fox_v_p", []).append(v.reshape(bp, seq, N_KV, HEAD_DIM))
            outs.setdefault("fox_lf_p", []).append(lf.reshape(bp, seq, N_HEADS))
            q, k, v, kh, vh, lf, c = _proj_qkv(xs, gm, w, b_f=b_fox_f[j], seq_len=1)
            o_s = _decode("fox", _repeat_q(q, N_KV), _kv_pages_t(cache_fox_k[j]), _kv_pages_t(cache_fox_v[j]),
                          page_table, pb=pb, lf_pool=jnp.transpose(cache_fox_logf[j], (0, 2, 1)),
                          k_new=k.reshape(bs, 1, KV_W), v_new=v.reshape(bs, 1, KV_W),
                          lf_new=lf.reshape(bs, N_HEADS, 1))
            o_s = o_s.reshape(bs, Q_W)
            outs.setdefault("fox_k_s", []).append(k.reshape(bs, 1, N_KV, HEAD_DIM))
            outs.setdefault("fox_v_s", []).append(v.reshape(bs, 1, N_KV, HEAD_DIM))
            outs.setdefault("fox_lf_s", []).append(lf.reshape(bs, 1, N_HEADS))
            w_o = bf(w_fox_o[j])
        else:
            w = bf(w_swa_qkv[j])
            q, k, v, kh, vh = _proj_qkv(xp, gm, w, rope_tabs=swa_tabs_p)
            o_p = _attn("swa", q, kh, vh, bp, seq, sinks=swa_sinks[j])
            outs.setdefault("swa_k_p", []).append(k.reshape(bp, seq, N_KV, HEAD_DIM)[:, seq - win:])
            outs.setdefault("swa_v_p", []).append(v.reshape(bp, seq, N_KV, HEAD_DIM)[:, seq - win:])
            q, k, v, kh, vh = _proj_qkv(xs, gm, w, rope_tabs=swa_tabs_s)
            o_s = _decode("swa", _repeat_q(q, N_KV), _kv_pages_t(state_swa_k[j]), _kv_pages_t(state_swa_v[j]),
                          ident_pages, pb=1, k_new=k.reshape(bs, 1, KV_W), v_new=v.reshape(bs, 1, KV_W),
                          sinks=swa_sinks[j])
            o_s = o_s.reshape(bs, Q_W)
            k_all = jnp.concatenate([state_swa_k[j], k.reshape(bs, 1, N_KV, HEAD_DIM)], axis=1)
            v_all = jnp.concatenate([state_swa_v[j], v.reshape(bs, 1, N_KV, HEAD_DIM)], axis=1)
            outs.setdefault("swa_k_s", []).append(k_all[:, -win:])
            outs.setdefault("swa_v_s", []).append(v_all[:, -win:])
            w_o = bf(w_swa_o[j])

        kv_f32, kv_b16 = _proj_plain(mem_flat, g_mem[i], bf(w_ckv[i]))
        mem_k_p.append(kv_f32[:, :MEM_W].reshape(bp, n_mem, MEM_HEADS, MEM_HEAD_DIM))
        mem_v_p.append(kv_f32[:, MEM_W:].reshape(bp, n_mem, MEM_HEADS, MEM_HEAD_DIM))
        w_cq_b, w_co_b = bf(w_cq[i]), bf(w_co[i])
        x1_p, oc_p = _oproj(xp, o_p, w_o, g_cross[i], w_cq_b,
                            mem=(kv_b16[:, :MEM_W], kv_b16[:, MEM_W:]), seq_len=seq)
        x1_s, qc_s = _oproj(xs, o_s, w_o, g_cross[i], w_cq_b)
        qc_rows = jnp.pad(qc_s.reshape(bs, MEM_HEADS, MEM_HEAD_DIM), ((0, 0), (0, SUBLANES - MEM_HEADS), (0, 0)))
        oc_s = _decode("cross", qc_rows, mem_k_rows, mem_v_rows, ident_pages + i * bs, pb=1)
        oc_s = oc_s[:, :MEM_HEADS].reshape(bs, MEM_W)

        fin = g_final if i == depth - 1 else None
        w_in_b, w_out_b = bf(w_ffn_in[i]), bf(w_ffn_out[i])
        res = _ffn(x1_p, oc_p, w_co_b, g_ffn[i], w_in_b, w_ffn_conv[i], b_ffn_conv[i], w_out_b,
                   seq_len=seq, g_final=fin)
        xp, tails = res[0], res[1]
        if fin is not None:
            y_p = res[2]
        tps = tails.shape[0] // bp
        conv_p.append(tails.reshape(bp, tps, SUBLANES, D_FF)[:, -1, SUBLANES - (CONV_W - 1):])
        buf = state_ffn_conv[i]
        res = _ffn(x1_s, oc_s, w_co_b, g_ffn[i], w_in_b, w_ffn_conv[i], b_ffn_conv[i], w_out_b,
                   buf=(buf[:, 0], buf[:, 1]), g_final=fin)
        xs, gate_s = res[0], res[1]
        if fin is not None:
            y_s = res[2]
        conv_s.append(jnp.stack([buf[:, 1], gate_s], axis=1))

    stk = lambda name: jnp.stack(outs[name])
    return (y_p.reshape(bp, seq, D_MODEL), y_s.reshape(bs, 1, D_MODEL),
            stk("sb_k_p"), stk("sb_v_p"), stk("sb_k_s"), stk("sb_v_s"),
            stk("mla_ckv_p"), stk("mla_kr_p"), stk("mla_ckv_s"), stk("mla_kr_s"),
            stk("fox_k_p"), stk("fox_v_p"), stk("fox_lf_p"),
            stk("fox_k_s"), stk("fox_v_s"), stk("fox_lf_s"),
            stk("swa_k_p"), stk("swa_v_p"), stk("swa_k_s"), stk("swa_v_s"),
            jnp.stack(mem_k_p), jnp.stack(mem_v_p),
            jnp.stack(conv_p), jnp.stack(conv_s))
```

```python
import functools
import math

import jax
import jax.numpy as jnp
from jax import lax
from jax.experimental import pallas as pl
from jax.experimental.pallas import tpu as pltpu

F32 = jnp.float32
BF16 = jnp.bfloat16

D_MODEL = 1024
N_HEADS = 16
HEAD_DIM = 64
N_KV = 4
GROUP = N_HEADS // N_KV
Q_W = N_HEADS * HEAD_DIM
KV_W = N_KV * HEAD_DIM
QKV_W = Q_W + 2 * KV_W
MLA_Q_RANK = 768
MLA_KV_RANK = 256
MLA_NOPE = 64
MLA_ROPE = 32
MLA_V = 64
MLA_QK = MLA_NOPE + MLA_ROPE
WINDOW = 128
MEM_HEADS = 4
MEM_HEAD_DIM = 128
MEM_W = MEM_HEADS * MEM_HEAD_DIM
D_FF = 2816
CONV_W = 3
ROPE_THETA = 10000.0
EPS = 1e-6
PAGE_SIZE = 128

LANES = 128
SUBLANES = 8
NEG_BIG = -1e30
VMEM_LIMIT = 56 * 1024 * 1024
CUM_CHUNK = 256
LOG2E = math.log2(math.e)
V_ONE = HEAD_DIM


def _cparams(*sem):
    return pltpu.CompilerParams(dimension_semantics=sem, vmem_limit_bytes=VMEM_LIMIT)


def _full(shape):
    return pl.BlockSpec(shape, lambda *_: (0,) * len(shape))


def _row_tile(m, pref):
    t = min(m, pref)
    assert m % t == 0, (m, t)
    return t


def _rms(x, g):
    return x * lax.rsqrt(jnp.mean(x * x, axis=-1, keepdims=True) + EPS) * g


def _dot(a, b):
    return jnp.dot(a, b, preferred_element_type=F32)


def _dot_nt(a, b):
    return lax.dot_general(a, b, (((1,), (1,)), ((), ())), preferred_element_type=F32)


def _log_sigmoid(z):
    return jnp.minimum(z, 0.0) - jnp.log1p(jnp.exp(-jnp.abs(z)))


def _log2_one_minus_sigmoid(z2):
    nz = -z2
    return jnp.minimum(nz, 0.0) - jnp.log2(1.0 + jnp.exp2(jnp.minimum(z2, nz)))


def _rope_slab(x, cos_t, sin_a, sin_b, half):
    return (x * cos_t + pltpu.roll(x, LANES - half, axis=1) * sin_a
            + pltpu.roll(x, half, axis=1) * sin_b)


def _rope_wide(x, cos_t, sin_a, sin_b, half):
    n = x.shape[1] // LANES
    return jnp.concatenate(
        [_rope_slab(x[:, i * LANES:(i + 1) * LANES], cos_t, sin_a, sin_b, half) for i in range(n)],
        axis=1)


def _split2(x):
    a = x.astype(BF16)
    b = (x - a.astype(F32)).astype(BF16)
    return a, b


def _split3(x):
    a = x.astype(BF16)
    r = x - a.astype(F32)
    b = r.astype(BF16)
    c = (r - b.astype(F32)).astype(BF16)
    return a, b, c


def _store_heads(dst_ref, a, n_heads, scale=None):
    for h in range(n_heads):
        blk = a[:, h * HEAD_DIM:(h + 1) * HEAD_DIM]
        if scale is not None:
            blk = blk * scale
        dst_ref[h] = blk.astype(dst_ref.dtype)


def _store_v_heads(dst_ref, v, n_heads):
    rows = v.shape[0]
    one_col = (lax.broadcasted_iota(jnp.int32, (rows, LANES - HEAD_DIM), 1) == 0).astype(F32)
    for h in range(n_heads):
        blk = jnp.concatenate([v[:, h * HEAD_DIM:(h + 1) * HEAD_DIM], one_col], axis=1)
        dst_ref[h] = blk.astype(dst_ref.dtype)


def _proj_qkv_body(*refs, rope, fox, nseq_tiles):
    it = iter(refs)
    x_ref, g_ref, w_ref = next(it), next(it), next(it)
    if rope:
        cos_ref, sa_ref, sb_ref = next(it), next(it), next(it)
    if fox:
        bf_ref, tri_ref = next(it), next(it)
    q_ref, k_ref, v_ref, kh_ref, vh_ref = next(it), next(it), next(it), next(it), next(it)
    if fox:
        lf_ref, c_ref, carry_sc = next(it), next(it), next(it)

    h = _rms(x_ref[...], g_ref[...]).astype(BF16)
    a = _dot(h, w_ref[...])
    q = a[:, :Q_W]
    k = a[:, Q_W:Q_W + KV_W]
    v = a[:, Q_W + KV_W:QKV_W]
    if rope:
        cos_t, sin_a, sin_b = cos_ref[...], sa_ref[...], sb_ref[...]
        q = _rope_wide(q, cos_t, sin_a, sin_b, HEAD_DIM // 2)
        k = _rope_wide(k, cos_t, sin_a, sin_b, HEAD_DIM // 2)
    k_ref[...] = k
    v_ref[...] = v
    _store_heads(q_ref, q, N_HEADS, scale=HEAD_DIM ** -0.5 * LOG2E)
    _store_heads(kh_ref, k, N_KV)
    _store_v_heads(vh_ref, v, N_KV)
    if fox:
        @pl.when(pl.program_id(0) % nseq_tiles == 0)
        def _():
            carry_sc[...] = jnp.zeros_like(carry_sc)

        lf = _log_sigmoid(a[:, QKV_W:QKV_W + N_HEADS] + bf_ref[...])
        lf_ref[...] = lf
        tri = tri_ref[...]
        t3 = _split3(lf)
        cs = _dot(tri, t3[0]) + _dot(tri, t3[1]) + _dot(tri, t3[2]) + carry_sc[...]
        c_ref[...] = cs * LOG2E
        carry_sc[...] = cs[cs.shape[0] - 1:, :]


def _proj_qkv(x, g, w, *, rope_tabs=None, b_f=None, seq_len=1):
    m = x.shape[0]
    rope = rope_tabs is not None
    fox = b_f is not None
    tm = _row_tile(min(m, seq_len) if fox and seq_len > 1 else m, 512)
    nseq_tiles = max(seq_len // tm, 1)
    in_specs = [pl.BlockSpec((tm, D_MODEL), lambda i: (i, 0)), _full((1, D_MODEL)), _full(w.shape)]
    args = [x, g.reshape(1, D_MODEL), w]
    if rope:
        nt = rope_tabs[0].shape[0] // tm
        for t in rope_tabs:
            in_specs.append(pl.BlockSpec((tm, LANES), lambda i, nt=nt: (i % nt, 0)))
            args.append(t)
    if fox:
        in_specs += [_full((1, N_HEADS)), _full((tm, tm))]
        args += [b_f.reshape(1, N_HEADS), jnp.tril(jnp.ones((tm, tm), F32)).astype(BF16)]
    out_shape = [
        jax.ShapeDtypeStruct((N_HEADS, m, HEAD_DIM), BF16),
        jax.ShapeDtypeStruct((m, KV_W), F32),
        jax.ShapeDtypeStruct((m, KV_W), F32),
        jax.ShapeDtypeStruct((N_KV, m, HEAD_DIM), BF16),
        jax.ShapeDtypeStruct((N_KV, m, LANES), BF16),
    ]
    out_specs = [
        pl.BlockSpec((N_HEADS, tm, HEAD_DIM), lambda i: (0, i, 0)),
        pl.BlockSpec((tm, KV_W), lambda i: (i, 0)),
        pl.BlockSpec((tm, KV_W), lambda i: (i, 0)),
        pl.BlockSpec((N_KV, tm, HEAD_DIM), lambda i: (0, i, 0)),
        pl.BlockSpec((N_KV, tm, LANES), lambda i: (0, i, 0)),
    ]
    scratch = []
    if fox:
        out_shape += [jax.ShapeDtypeStruct((m, N_HEADS), F32)] * 2
        out_specs += [pl.BlockSpec((tm, N_HEADS), lambda i: (i, 0))] * 2
        scratch = [pltpu.VMEM((1, N_HEADS), F32)]
    return pl.pallas_call(
        functools.partial(_proj_qkv_body, rope=rope, fox=fox, nseq_tiles=nseq_tiles),
        grid=(m // tm,), in_specs=in_specs, out_specs=out_specs, out_shape=out_shape,
        scratch_shapes=scratch, compiler_params=_cparams("arbitrary"),
        name="proj_fox" if fox else ("proj_swa" if rope else "proj_sb"),
    )(*args)


def _proj_mla_body(x_ref, g_ref, win_ref, gq_ref, gkv_ref, wuq_ref, wuk_ref, wuv_ref, wukt_ref,
                   cos_ref, sa_ref, sb_ref,
                   q_ref, kh_ref, vh_ref, ckv_ref, kr_ref, *maybe_qlat, absorbed):
    scale = MLA_QK ** -0.5 * LOG2E
    half = MLA_ROPE // 2
    cos_t, sin_a, sin_b = cos_ref[...], sa_ref[...], sb_ref[...]
    h = _rms(x_ref[...], g_ref[...]).astype(BF16)
    a = _dot(h, win_ref[...])
    cq = _rms(a[:, :MLA_Q_RANK], gq_ref[...]).astype(BF16)
    ckv = _rms(a[:, MLA_Q_RANK:MLA_Q_RANK + MLA_KV_RANK], gkv_ref[...])
    ckv_ref[...] = ckv
    ckv_b = ckv.astype(BF16)
    kr = _rope_slab(a[:, MLA_Q_RANK + MLA_KV_RANK:], cos_t, sin_a, sin_b, half)
    kr_ref[...] = kr
    qf = _dot(cq, wuq_ref[...])
    kn = _dot(ckv_b, wuk_ref[...])
    _store_v_heads(vh_ref, _dot(ckv_b, wuv_ref[...]), N_HEADS)
    for hd in range(N_HEADS):
        sl = slice(hd * LANES, (hd + 1) * LANES)
        qh = _rope_slab(qf[:, sl], cos_t, sin_a, sin_b, half) * scale
        q_ref[hd] = qh.astype(BF16)
        kh_ref[hd] = (kn[:, sl] + kr).astype(BF16)
        if absorbed:
            qn = qh[:, :MLA_NOPE].astype(BF16)
            maybe_qlat[0][hd] = _dot(qn, wukt_ref[hd])


def _proj_mla(x, g, wp, rope_tabs, *, absorbed):
    m = x.shape[0]
    tm = _row_tile(m, 512)
    nt = rope_tabs[0].shape[0] // tm
    args = [x, g.reshape(1, D_MODEL), wp["w_in"], wp["g_q"], wp["g_kv"], wp["w_uq"], wp["w_uk"],
            wp["w_uv"], wp["w_ukt"]]
    in_specs = [pl.BlockSpec((tm, D_MODEL), lambda i: (i, 0))] + [_full(a.shape) for a in args[1:]]
    for t in rope_tabs:
        in_specs.append(pl.BlockSpec((tm, LANES), lambda i, nt=nt: (i % nt, 0)))
        args.append(t)
    out_shape = [
        jax.ShapeDtypeStruct((N_HEADS, m, LANES), BF16),
        jax.ShapeDtypeStruct((N_HEADS, m, LANES), BF16),
        jax.ShapeDtypeStruct((N_HEADS, m, LANES), BF16),
        jax.ShapeDtypeStruct((m, MLA_KV_RANK), F32),
        jax.ShapeDtypeStruct((m, LANES), F32),
    ]
    hm = pl.BlockSpec((N_HEADS, tm, LANES), lambda i: (0, i, 0))
    out_specs = [hm, hm, hm,
                 pl.BlockSpec((tm, MLA_KV_RANK), lambda i: (i, 0)),
                 pl.BlockSpec((tm, LANES), lambda i: (i, 0))]
    if absorbed:
        out_shape.append(jax.ShapeDtypeStruct((N_HEADS, m, MLA_KV_RANK), F32))
        out_specs.append(pl.BlockSpec((N_HEADS, tm, MLA_KV_RANK), lambda i: (0, i, 0)))
    return pl.pallas_call(
        functools.partial(_proj_mla_body, absorbed=absorbed),
        grid=(m // tm,), in_specs=in_specs, out_specs=out_specs, out_shape=out_shape,
        compiler_params=_cparams("arbitrary"), name="proj_mla",
    )(*args)


def _proj_plain_body(x_ref, g_ref, w_ref, o_ref, ob_ref):
    a = _dot(_rms(x_ref[...], g_ref[...]).astype(BF16), w_ref[...])
    o_ref[...] = a
    ob_ref[...] = a.astype(BF16)


def _proj_plain(x, g, w):
    m, n = x.shape[0], w.shape[1]
    tm = _row_tile(m, 512)
    return pl.pallas_call(
        _proj_plain_body, grid=(m // tm,),
        in_specs=[pl.BlockSpec((tm, D_MODEL), lambda i: (i, 0)), _full((1, D_MODEL)), _full(w.shape)],
        out_specs=[pl.BlockSpec((tm, n), lambda i: (i, 0))] * 2,
        out_shape=[jax.ShapeDtypeStruct((m, n), F32), jax.ShapeDtypeStruct((m, n), BF16)],
        compiler_params=_cparams("arbitrary"), name="proj_mem",
    )(x, g.reshape(1, D_MODEL), w)


FLAG_FIRST, FLAG_LAST, FLAG_MASK = 1, 2, 4


def _schedule(n_tiles, tile, *, window=None, descending=False, strict=False):
    qi, kj, fl = [], [], []
    for i in range(n_tiles):
        q_lo = i * tile
        lo = 0 if window is None else max(0, (q_lo - window) // tile)
        js = list(range(lo, i + 1))
        if descending:
            js = js[::-1]
        for n, j in enumerate(js):
            k_hi = (j + 1) * tile - 1
            need_mask = (k_hi >= q_lo) if strict else (k_hi > q_lo)
            if window is not None:
                need_mask = True
            f = (FLAG_FIRST if n == 0 else 0) | (FLAG_LAST if n == len(js) - 1 else 0)
            f |= FLAG_MASK if need_mask else 0
            qi.append(i), kj.append(j), fl.append(f)
    return (jnp.asarray(qi, jnp.int32), jnp.asarray(kj, jnp.int32), jnp.asarray(fl, jnp.int32))


def _attn_body(qi_ref, kj_ref, fl_ref, *refs, mode, nh, tq, tk):
    it = iter(refs)
    if mode == "swa":
        sink_ref = next(it)
    q_ref, k_ref, v_ref = next(it), next(it), next(it)
    if mode == "fox":
        cq_ref, ck_ref = next(it), next(it)
    o_ref, m_sc, acc_sc = next(it), next(it), next(it)
    if mode == "fox":
        cqr_sc = next(it)
    g = pl.program_id(1)
    s_id = pl.program_id(2)
    flag = fl_ref[s_id]
    q_lo = qi_ref[s_id] * tq
    k_lo = kj_ref[s_id] * tk
    own_kv = mode == "mla"

    @pl.when((flag & FLAG_FIRST) != 0)
    def _():
        if mode == "swa":
            one_col = (lax.broadcasted_iota(jnp.int32, (tq, LANES), 1) == V_ONE).astype(F32)
            for hh in range(nh):
                m_sc[hh] = jnp.full((tq, LANES), sink_ref[g * nh + hh] * LOG2E, F32)
                acc_sc[hh] = one_col
        else:
            m_sc[...] = jnp.full(m_sc.shape, NEG_BIG, F32)
            acc_sc[...] = jnp.zeros_like(acc_sc)
        if mode == "fox":
            for hh in range(nh):
                cqr_sc[hh] = jnp.broadcast_to(cq_ref[0][:, hh:hh + 1], (tq, LANES))

    def step(masked):
        if masked:
            qpos = q_lo + lax.broadcasted_iota(jnp.int32, (tq, tk), 0)
            kpos = k_lo + lax.broadcasted_iota(jnp.int32, (tq, tk), 1)
            vis = kpos <= qpos
            if mode == "swa":
                vis = vis & (qpos - kpos <= WINDOW)
        for hh in range(nh):
            k = k_ref[hh] if own_kv else k_ref[0]
            v = v_ref[hh] if own_kv else v_ref[0]
            s = _dot_nt(q_ref[hh], k)
            if mode == "fox":
                s = s - ck_ref[0][hh:hh + 1, :]
            if masked:
                s = jnp.where(vis, s, NEG_BIG)
            m_prev = m_sc[hh]
            smax = jnp.max(s, axis=-1, keepdims=True)
            if mode == "fox":
                cq = cqr_sc[hh]
                m_new = jnp.maximum(m_prev, smax + cq)
                shift = m_new - cq
            else:
                m_new = jnp.maximum(m_prev, smax)
                shift = m_new
            alpha = jnp.exp2(m_prev - m_new)
            p = jnp.exp2(s - jnp.concatenate([shift] * (tk // LANES), axis=1))
            acc_sc[hh] = alpha * acc_sc[hh] + _dot(p.astype(BF16), v)
            m_sc[hh] = m_new

    if mode == "swa":
        step(True)
    else:
        @pl.when((flag & FLAG_MASK) != 0)
        def _():
            step(True)

        @pl.when((flag & FLAG_MASK) == 0)
        def _():
            step(False)

    @pl.when((flag & FLAG_LAST) != 0)
    def _():
        for hh in range(nh):
            acc = acc_sc[hh]
            o = acc[:, :HEAD_DIM] / acc[:, V_ONE:V_ONE + 1]
            o_ref[:, hh * HEAD_DIM:(hh + 1) * HEAD_DIM] = o.astype(o_ref.dtype)


def _attn(mode, q_hm, k_hm, v_hm, batch, seq, *, cq=None, ck=None, sinks=None):
    nh = GROUP
    tq = tk = _row_tile(seq, 512 if mode == "swa" else 1024)
    nq = seq // tq
    dk = q_hm.shape[-1]
    nkv = nh if mode == "mla" else 1
    qi, kj, fl = _schedule(nq, tq, window=WINDOW if mode == "swa" else None)
    prefetch = [qi, kj, fl]
    if mode == "swa":
        prefetch.append(sinks.astype(F32))

    def qmap(b, g, s, qi, kj, *_):
        return (g, b * nq + qi[s], 0)

    def kmap(b, g, s, qi, kj, *_):
        return (g, b * nq + kj[s], 0)

    in_specs = [pl.BlockSpec((nh, tq, dk), qmap), pl.BlockSpec((nkv, tk, dk), kmap),
                pl.BlockSpec((nkv, tk, LANES), kmap)]
    args = [q_hm, k_hm, v_hm]
    if mode == "fox":
        in_specs.append(pl.BlockSpec((1, tq, GROUP), qmap))
        in_specs.append(pl.BlockSpec((1, GROUP, tk), lambda b, g, s, qi, kj, *_: (g, 0, b * nq + kj[s])))
        args += [cq, ck]
    grid_spec = pltpu.PrefetchScalarGridSpec(
        num_scalar_prefetch=len(prefetch), grid=(batch, N_HEADS // nh, int(qi.shape[0])),
        in_specs=in_specs,
        out_specs=pl.BlockSpec((tq, nh * HEAD_DIM), lambda b, g, s, qi, kj, *_: (b * nq + qi[s], g)),
        scratch_shapes=[pltpu.VMEM((nh, tq, LANES), F32)] * (3 if mode == "fox" else 2))
    return pl.pallas_call(
        functools.partial(_attn_body, mode=mode, nh=nh, tq=tq, tk=tk),
        grid_spec=grid_spec, out_shape=jax.ShapeDtypeStruct((batch * seq, Q_W), BF16),
        compiler_params=_cparams("arbitrary", "arbitrary", "arbitrary"), name="attn_" + mode,
    )(*prefetch, *args)


def _attn_sb_body(qi_ref, kj_ref, fl_ref, q_ref, k_ref, v_ref, u_ref, o_ref, run_sc, acc_sc, *, nh, tq, tk):
    s_id = pl.program_id(2)
    flag = fl_ref[s_id]
    q_lo = qi_ref[s_id] * tq
    k_lo = kj_ref[s_id] * tk
    ck = min(CUM_CHUNK, tk)
    n_chunks = tk // ck

    @pl.when((flag & FLAG_FIRST) != 0)
    def _():
        run_sc[...] = jnp.zeros_like(run_sc)
        acc_sc[...] = jnp.zeros_like(acc_sc)

    def step(masked):
        u = u_ref[...]
        for r in reversed(range(n_chunks)):
            kc = k_ref[0, r * ck:(r + 1) * ck, :]
            vc = v_ref[0, r * ck:(r + 1) * ck, :]
            if masked:
                qpos = q_lo + lax.broadcasted_iota(jnp.int32, (tq, ck), 0)
                kpos = k_lo + r * ck + lax.broadcasted_iota(jnp.int32, (tq, ck), 1)
                vis = kpos < qpos
            for hh in range(nh):
                z = _dot_nt(q_ref[hh], kc)
                lr = _log2_one_minus_sigmoid(z)
                if masked:
                    lr = jnp.where(vis, lr, 0.0)
                incl = _dot(jnp.concatenate(_split2(lr), axis=1), u)
                run = run_sc[hh]
                w = jnp.exp2(z + incl + jnp.concatenate([run] * (ck // LANES), axis=1))
                if masked:
                    w = jnp.where(vis, w, 0.0)
                acc_sc[hh] = acc_sc[hh] + _dot(w.astype(BF16), vc)
                run_sc[hh] = run + jnp.broadcast_to(incl[:, 0:1], (tq, LANES))

    @pl.when((flag & FLAG_MASK) != 0)
    def _():
        step(True)

    @pl.when((flag & FLAG_MASK) == 0)
    def _():
        step(False)

    @pl.when((flag & FLAG_LAST) != 0)
    def _():
        for hh in range(nh):
            o_ref[:, hh * HEAD_DIM:(hh + 1) * HEAD_DIM] = acc_sc[hh][:, :HEAD_DIM].astype(o_ref.dtype)


def _attn_sb(q_hm, k_hm, v_hm, batch, seq):
    nh = GROUP
    tq = tk = _row_tile(seq, 1024)
    nq = seq // tq
    ck = min(CUM_CHUNK, tk)
    qi, kj, fl = _schedule(nq, tq, descending=True, strict=True)
    u = (jnp.arange(ck)[:, None] >= jnp.arange(ck)[None, :]).astype(BF16)
    u = jnp.concatenate([u, u], axis=0)

    def qmap(b, g, s, qi, kj, fl):
        return (g, b * nq + qi[s], 0)

    def kmap(b, g, s, qi, kj, fl):
        return (g, b * nq + kj[s], 0)

    grid_spec = pltpu.PrefetchScalarGridSpec(
        num_scalar_prefetch=3, grid=(batch, N_KV, int(qi.shape[0])),
        in_specs=[pl.BlockSpec((nh, tq, HEAD_DIM), qmap), pl.BlockSpec((1, tk, HEAD_DIM), kmap),
                  pl.BlockSpec((1, tk, LANES), kmap), pl.BlockSpec((2 * ck, ck), lambda *_: (0, 0))],
        out_specs=pl.BlockSpec((tq, nh * HEAD_DIM), lambda b, g, s, qi, kj, fl: (b * nq + qi[s], g)),
        scratch_shapes=[pltpu.VMEM((nh, tq, LANES), F32), pltpu.VMEM((nh, tq, LANES), F32)])
    return pl.pallas_call(
        functools.partial(_attn_sb_body, nh=nh, tq=tq, tk=tk),
        grid_spec=grid_spec, out_shape=jax.ShapeDtypeStruct((batch * seq, Q_W), BF16),
        compiler_params=_cparams("arbitrary", "arbitrary", "arbitrary"), name="attn_sb",
    )(qi, kj, fl, q_hm, k_hm, v_hm, u)


def _suffix_sum_lanes(x):
    h, w = x.shape
    n = w // LANES
    lane = lax.broadcasted_iota(jnp.int32, (h, LANES), 1)
    slabs = []
    for i in range(n):
        y = x[:, i * LANES:(i + 1) * LANES]
        d = 1
        while d < LANES:
            y = y + jnp.where(lane < LANES - d, pltpu.roll(y, LANES - d, axis=1), 0.0)
            d *= 2
        slabs.append(y)
    tots = [jnp.broadcast_to(y[:, 0:1], (h, LANES)) for y in slabs]
    run = jnp.zeros((h, LANES), F32)
    out = [None] * n
    for i in reversed(range(n)):
        out[i] = slabs[i] + run
        run = run + tots[i]
    return (jnp.concatenate(out, axis=1) if n > 1 else out[0]), run


def _decode_body(pt_ref, *refs, mode, pb, hq, new_key):
    it = iter(refs)
    q_ref = next(it)
    qr_ref = next(it) if mode == "mla" else None
    n_pools = 3 if mode == "fox" else 2
    pools = [next(it) for _ in range(n_pools)]
    if new_key:
        kn_ref = next(it)
        krn_ref = next(it) if mode == "mla" else None
        vn_ref = kn_ref if mode == "mla" else next(it)
        lfn_ref = next(it) if mode == "fox" else None
    sink_ref = next(it) if mode == "swa" else None
    o_ref, m_sc, l_sc, acc_sc, car_sc = next(it), next(it), next(it), next(it), next(it)
    bufs = [next(it) for _ in range(n_pools)]
    sem = next(it)
    b = pl.program_id(0)
    t = pl.program_id(1)
    nt = pl.num_programs(1)
    n_pages = nt * pb
    step = b * nt + t
    slot = step % 2

    def page_copies(bb, tt, sl):
        out = []
        for r in range(pb):
            page = pt_ref[bb * n_pages + n_pages - 1 - (tt * pb + r)]
            for n in range(n_pools):
                out.append(pltpu.make_async_copy(pools[n].at[page], bufs[n].at[sl, r], sem.at[sl, n]))
        return out

    @pl.when(step == 0)
    def _():
        for c in page_copies(0, 0, 0):
            c.start()

    @pl.when(step + 1 < pl.num_programs(0) * nt)
    def _():
        last_t = t + 1 == nt
        for c in page_copies(jnp.where(last_t, b + 1, b), jnp.where(last_t, 0, t + 1), 1 - slot):
            c.start()

    for c in page_copies(b, t, slot):
        c.wait()

    k_refs = [bufs[0].at[slot, r] for r in range(pb)]
    kr_refs = [bufs[1].at[slot, r] for r in range(pb)] if mode == "mla" else None
    v_refs = k_refs if mode == "mla" else [bufs[1].at[slot, r] for r in range(pb)]
    lf_refs = [bufs[2].at[slot, r] for r in range(pb)] if mode == "fox" else None
    dk = q_ref.shape[-1]
    grouped = mode in ("sb", "fox", "swa")
    softmax = mode != "sb"

    if grouped:
        row_kv = lax.broadcasted_iota(jnp.int32, (hq, dk), 0) // GROUP
        lane_kv = lax.broadcasted_iota(jnp.int32, (hq, dk), 1) // HEAD_DIM
        q = jnp.where(row_kv == lane_kv, q_ref[0], 0.0)
    else:
        q = q_ref[0]
    qb = q.astype(BF16)

    @pl.when(t == 0)
    def _():
        if mode == "swa":
            m_sc[...] = jnp.broadcast_to(sink_ref[...] * LOG2E, m_sc.shape)
            l_sc[...] = jnp.ones_like(l_sc)
        else:
            m_sc[...] = jnp.full(m_sc.shape, NEG_BIG, F32)
            l_sc[...] = jnp.zeros_like(l_sc)
        acc_sc[...] = jnp.zeros_like(acc_sc)
        if mode == "fox" and new_key:
            car_sc[...] = jnp.broadcast_to(lfn_ref[0] * LOG2E, car_sc.shape)
        else:
            car_sc[...] = jnp.zeros_like(car_sc)

    order = list(reversed(range(pb)))
    s_parts = []
    for r in order:
        kp = k_refs[r][...].astype(BF16)
        if grouped:
            s = _dot(qb, kp)
        else:
            s = _dot_nt(qb, kp)
        if mode == "mla":
            s = s + _dot(qr_ref[0].astype(BF16), kr_refs[r][...].astype(BF16))
        s_parts.append(s)
    s = jnp.concatenate(s_parts, axis=1) if pb > 1 else s_parts[0]
    ps = s.shape[1] // pb
    nrep = s.shape[1] // LANES

    def wide(x):
        return jnp.concatenate([x] * nrep, axis=1) if nrep > 1 else x

    def pv(p):
        pb16 = p.astype(BF16)
        out = None
        for n, r in enumerate(order):
            blk = pb16[:, n * ps:(n + 1) * ps]
            vp = v_refs[r][...].astype(BF16)
            term = _dot_nt(blk, vp) if grouped else _dot(blk, vp)
            out = term if out is None else out + term
        return out

    if mode == "sb":
        lr = _log2_one_minus_sigmoid(s)
        incl, tot = _suffix_sum_lanes(lr)
        w = jnp.exp2(s + incl + wide(car_sc[...]))
        acc_sc[...] = acc_sc[...] + pv(w)
        car_sc[...] = car_sc[...] + tot
    else:
        if mode == "fox":
            lf = jnp.concatenate([lf_refs[r][...] for r in order], axis=1) if pb > 1 else lf_refs[0][...]
            lf = lf * LOG2E
            incl, tot = _suffix_sum_lanes(lf)
            s = s + (incl - lf) + wide(car_sc[...])
            car_sc[...] = car_sc[...] + tot
        m_prev = m_sc[...]
        m_new = jnp.maximum(m_prev, jnp.max(s, axis=-1, keepdims=True))
        alpha = jnp.exp2(m_prev - m_new)
        p = jnp.exp2(s - wide(m_new))
        l_sc[...] = alpha * l_sc[...] + jnp.sum(p, axis=-1, keepdims=True)
        nacc = acc_sc.shape[-1] // LANES
        alpha_w = jnp.concatenate([alpha] * nacc, axis=1) if nacc > 1 else alpha
        acc_sc[...] = alpha_w * acc_sc[...] + pv(p)
        m_sc[...] = m_new

    @pl.when(t == pl.num_programs(1) - 1)
    def _():
        acc = acc_sc[...]
        if softmax:
            m_prev, l_prev = m_sc[...][:, :1], l_sc[...][:, :1]
            if new_key:
                s_new = jnp.sum(q * kn_ref[0], axis=-1, keepdims=True)
                if mode == "mla":
                    s_new = s_new + jnp.sum(qr_ref[0] * krn_ref[0], axis=-1, keepdims=True)
                m_new = jnp.maximum(m_prev, s_new)
                alpha = jnp.exp2(m_prev - m_new)
                p_new = jnp.exp2(s_new - m_new)
                l_prev = alpha * l_prev + p_new
                acc = alpha * acc + p_new * vn_ref[0]
            acc = acc / l_prev
        if grouped:
            row_kv = lax.broadcasted_iota(jnp.int32, (hq, HEAD_DIM), 0) // GROUP
            out = jnp.zeros((hq, HEAD_DIM), F32)
            for kv in range(N_KV):
                out = out + jnp.where(row_kv == kv, acc[:, kv * HEAD_DIM:(kv + 1) * HEAD_DIM], 0.0)
            o_ref[0] = out
        else:
            o_ref[0] = acc


def _decode(mode, q, k_pool, v_pool, page_table, *, pb, qr=None, kr_pool=None,
            lf_pool=None, k_new=None, v_new=None, kr_new=None, lf_new=None, sinks=None):
    bsz, hq, dk = q.shape
    n_pages = page_table.shape[1]
    assert n_pages % pb == 0
    grouped = mode in ("sb", "fox", "swa")
    vblk = (v_pool if v_pool is not None else k_pool).shape[1:]
    dvt = vblk[0] if grouped else vblk[1]
    out_d = HEAD_DIM if grouped else dvt
    new_key = k_new is not None
    pt = page_table.reshape(-1).astype(jnp.int32)

    def seq_map(b, t, pt):
        return (b, 0, 0)

    in_specs = [pl.BlockSpec((1, hq, dk), seq_map)]
    args = [q]
    if mode == "mla":
        in_specs.append(pl.BlockSpec((1, hq, qr.shape[-1]), seq_map))
        args.append(qr)
    pools = [k_pool, kr_pool if mode == "mla" else v_pool] + ([lf_pool] if mode == "fox" else [])
    for p in pools:
        in_specs.append(pl.BlockSpec(memory_space=pl.ANY))
        args.append(p)
    if new_key:
        in_specs.append(pl.BlockSpec((1, 1, dk), seq_map))
        args.append(k_new)
        if mode == "mla":
            in_specs.append(pl.BlockSpec((1, 1, kr_new.shape[-1]), seq_map))
            args.append(kr_new)
        else:
            in_specs.append(pl.BlockSpec((1, 1, dvt), seq_map))
            args.append(v_new)
        if mode == "fox":
            in_specs.append(pl.BlockSpec((1, hq, 1), seq_map))
            args.append(lf_new)
    if mode == "swa":
        in_specs.append(pl.BlockSpec((hq, 1), lambda b, t, pt: (0, 0)))
        args.append(sinks.reshape(hq, 1).astype(F32))
    grid_spec = pltpu.PrefetchScalarGridSpec(
        num_scalar_prefetch=1, grid=(bsz, n_pages // pb), in_specs=in_specs,
        out_specs=pl.BlockSpec((1, hq, out_d), seq_map),
        scratch_shapes=[pltpu.VMEM((hq, LANES), F32), pltpu.VMEM((hq, LANES), F32),
                        pltpu.VMEM((hq, dvt), F32), pltpu.VMEM((hq, LANES), F32)]
        + [pltpu.VMEM((2, pb) + p.shape[1:], p.dtype) for p in pools]
        + [pltpu.SemaphoreType.DMA((2, len(pools)))])
    return pl.pallas_call(
        functools.partial(_decode_body, mode=mode, pb=pb, hq=hq, new_key=new_key),
        grid_spec=grid_spec, out_shape=jax.ShapeDtypeStruct((bsz, hq, out_d), F32),
        compiler_params=_cparams("arbitrary", "arbitrary"), name="decode_" + mode,
    )(pt, *args)


def _cross_decode_body(q_ref, k_ref, v_ref, o_ref, *, nseq):
    rows = k_ref.shape[1]
    col_head = lax.broadcasted_iota(jnp.int32, (SUBLANES, rows), 1) % MEM_HEADS
    own = col_head == lax.broadcasted_iota(jnp.int32, (SUBLANES, rows), 0)
    for n in range(nseq):
        s = _dot_nt(q_ref[n].astype(BF16), k_ref[n].astype(BF16))
        s = jnp.where(own, s, NEG_BIG)
        p = jnp.exp2(s - jnp.max(s, axis=-1, keepdims=True))
        o = _dot(p.astype(BF16), v_ref[n].astype(BF16))
        o_ref[n] = o / jnp.sum(p, axis=-1, keepdims=True)


def _cross_decode(q, k_rows, v_rows, layer):
    bsz = q.shape[0]
    nseq = math.gcd(bsz, 8)
    steps = bsz // nseq
    blk = (nseq,) + k_rows.shape[1:]
    return pl.pallas_call(
        functools.partial(_cross_decode_body, nseq=nseq), grid=(steps,),
        in_specs=[pl.BlockSpec((nseq, SUBLANES, MEM_HEAD_DIM), lambda i: (i, 0, 0)),
                  pl.BlockSpec(blk, lambda i: (layer * steps + i, 0, 0)),
                  pl.BlockSpec(blk, lambda i: (layer * steps + i, 0, 0))],
        out_specs=pl.BlockSpec((nseq, SUBLANES, MEM_HEAD_DIM), lambda i: (i, 0, 0)),
        out_shape=jax.ShapeDtypeStruct((bsz, SUBLANES, MEM_HEAD_DIM), F32),
        compiler_params=_cparams("arbitrary"), name="decode_cross",
    )(q, k_rows, v_rows)


def _heads_matmul_body(x_ref, w_ref, o_ref):
    o_ref[0] = _dot(x_ref[0].astype(BF16), w_ref[0])


def _heads_matmul(x, w):
    h, m, k = x.shape
    n = w.shape[-1]
    return pl.pallas_call(
        _heads_matmul_body, grid=(h,),
        in_specs=[pl.BlockSpec((1, m, k), lambda i: (i, 0, 0)), pl.BlockSpec((1, k, n), lambda i: (i, 0, 0))],
        out_specs=pl.BlockSpec((1, m, n), lambda i: (i, 0, 0)),
        out_shape=jax.ShapeDtypeStruct((h, m, n), F32),
        compiler_params=_cparams("arbitrary"), name="mla_uv",
    )(x, w)


def _oproj_body(x_ref, o_ref, wo_ref, gc_ref, wcq_ref, *refs, cross):
    if cross:
        mk_ref, mv_ref, x1_ref, oc_ref = refs
    else:
        x1_ref, qc_ref = refs
    x1 = x_ref[...] + _dot(o_ref[...].astype(BF16), wo_ref[...])
    x1_ref[...] = x1
    qc = _dot(_rms(x1, gc_ref[...]).astype(BF16), wcq_ref[...]) * (MEM_HEAD_DIM ** -0.5 * LOG2E)
    if not cross:
        qc_ref[...] = qc
        return
    qc = qc.astype(BF16)
    for hh in range(MEM_HEADS):
        sl = slice(hh * MEM_HEAD_DIM, (hh + 1) * MEM_HEAD_DIM)
        s = _dot_nt(qc[:, sl], mk_ref[:, sl])
        p = jnp.exp2(s - jnp.max(s, axis=-1, keepdims=True))
        o = _dot(p.astype(BF16), mv_ref[:, sl]) / jnp.sum(p, axis=-1, keepdims=True)
        oc_ref[:, sl] = o.astype(oc_ref.dtype)


def _oproj(x, o, w_o, g_c, w_cq, *, mem=None, seq_len=1):
    m = x.shape[0]
    tm = _row_tile(min(m, seq_len) if mem is not None else m, 512)
    cross = mem is not None
    in_specs = [pl.BlockSpec((tm, D_MODEL), lambda i: (i, 0)), pl.BlockSpec((tm, o.shape[1]), lambda i: (i, 0)),
                _full(w_o.shape), _full((1, D_MODEL)), _full(w_cq.shape)]
    args = [x, o, w_o, g_c.reshape(1, D_MODEL), w_cq]
    if cross:
        n_mem = mem[0].shape[0] // (m // seq_len)
        tiles_per_seq = seq_len // tm
        for a in mem:
            in_specs.append(pl.BlockSpec((n_mem, MEM_W), lambda i: (i // tiles_per_seq, 0)))
            args.append(a)
    out_shape = [jax.ShapeDtypeStruct((m, D_MODEL), F32),
                 jax.ShapeDtypeStruct((m, MEM_W), BF16 if cross else F32)]
    out_specs = [pl.BlockSpec((tm, D_MODEL), lambda i: (i, 0)), pl.BlockSpec((tm, MEM_W), lambda i: (i, 0))]
    return pl.pallas_call(
        functools.partial(_oproj_body, cross=cross), grid=(m // tm,),
        in_specs=in_specs, out_specs=out_specs, out_shape=out_shape,
        compiler_params=_cparams("arbitrary"), name="oproj_cross" if cross else "oproj",
    )(*args)


def _ffn_body(*refs, seq_mode, final, nseq_tiles, tm, fc):
    it = iter(refs)
    x1_ref, oc_ref, wco_ref, gf_ref = next(it), next(it), next(it), next(it)
    wg_ref, wv_ref, wc_ref, bc_ref, wout_ref = next(it), next(it), next(it), next(it), next(it)
    if not seq_mode:
        b0_ref, b1_ref = next(it), next(it)
    if final:
        gfin_ref = next(it)
    x3_ref, gate_ref = next(it), next(it)
    if final:
        y_ref = next(it)
    x2_sc, h_sc, acc_sc = next(it), next(it), next(it)
    if seq_mode:
        halo_sc, gs_sc = next(it), next(it)
    i = pl.program_id(0)
    f = pl.program_id(1)

    @pl.when(f == 0)
    def _():
        x2 = x1_ref[...] + _dot(oc_ref[...].astype(BF16), wco_ref[...])
        x2_sc[...] = x2
        h_sc[...] = _rms(x2, gf_ref[...]).astype(BF16)
        acc_sc[...] = jnp.zeros_like(acc_sc)

    h = h_sc[...]
    gate = _dot(h, wg_ref[...])
    val = _dot(h, wv_ref[...])
    wc = wc_ref[...]
    if seq_mode:
        @pl.when(i % nseq_tiles == 0)
        def _():
            gs_sc[0:SUBLANES, :] = jnp.zeros((SUBLANES, fc), F32)

        @pl.when(i % nseq_tiles != 0)
        def _():
            gs_sc[0:SUBLANES, :] = halo_sc[f]

        gs_sc[SUBLANES:, :] = gate
        gm1 = gs_sc[SUBLANES - 1:SUBLANES - 1 + tm, :]
        gm2 = gs_sc[SUBLANES - 2:SUBLANES - 2 + tm, :]
        tail = gate[tm - SUBLANES:, :]
        halo_sc[f] = tail
        gate_ref[0] = tail
    else:
        gm2, gm1 = b0_ref[...], b1_ref[...]
        gate_ref[...] = gate
    conv = bc_ref[...] + wc[0:1, :] * gm2 + wc[1:2, :] * gm1 + wc[2:3, :] * gate
    y = 0.5 * conv * (1.0 + lax.erf(conv * (2.0 ** -0.5))) * val
    acc_sc[...] = acc_sc[...] + _dot(y.astype(BF16), wout_ref[...])

    @pl.when(f == pl.num_programs(1) - 1)
    def _():
        x3 = x2_sc[...] + acc_sc[...]
        x3_ref[...] = x3
        if final:
            y_ref[...] = _rms(x3, gfin_ref[...])


def _ffn(x1, oc, w_co, g_f, w_in, w_conv, b_conv, w_out, *, seq_len=1, buf=None, g_final=None):
    m = x1.shape[0]
    seq_mode = buf is None
    final = g_final is not None
    tm = _row_tile(min(m, seq_len) if seq_mode else m, 512)
    nseq_tiles = max(seq_len // tm, 1)
    fc = D_FF // 2
    nf = D_FF // fc
    row = lambda i, f: (i, 0)
    in_specs = [pl.BlockSpec((tm, D_MODEL), row), pl.BlockSpec((tm, MEM_W), row), _full(w_co.shape),
                _full((1, D_MODEL)),
                pl.BlockSpec((D_MODEL, fc), lambda i, f: (0, f)),
                pl.BlockSpec((D_MODEL, fc), lambda i, f: (0, nf + f)),
                pl.BlockSpec((CONV_W, fc), lambda i, f: (0, f)),
                pl.BlockSpec((1, fc), lambda i, f: (0, f)),
                pl.BlockSpec((fc, D_MODEL), lambda i, f: (f, 0))]
    args = [x1, oc, w_co, g_f.reshape(1, D_MODEL), w_in, w_in, w_conv, b_conv.reshape(1, D_FF), w_out]
    if not seq_mode:
        in_specs += [pl.BlockSpec((tm, fc), lambda i, f: (i, f))] * 2
        args += [buf[0], buf[1]]
    if final:
        in_specs.append(_full((1, D_MODEL)))
        args.append(g_final.reshape(1, D_MODEL))
    out_shape = [jax.ShapeDtypeStruct((m, D_MODEL), F32)]
    out_specs = [pl.BlockSpec((tm, D_MODEL), row)]
    if seq_mode:
        out_shape.append(jax.ShapeDtypeStruct((m // tm, SUBLANES, D_FF), F32))
        out_specs.append(pl.BlockSpec((1, SUBLANES, fc), lambda i, f: (i, 0, f)))
    else:
        out_shape.append(jax.ShapeDtypeStruct((m, D_FF), F32))
        out_specs.append(pl.BlockSpec((tm, fc), lambda i, f: (i, f)))
    if final:
        out_shape.append(jax.ShapeDtypeStruct((m, D_MODEL), F32))
        out_specs.append(pl.BlockSpec((tm, D_MODEL), row))
    scratch = [pltpu.VMEM((tm, D_MODEL), F32), pltpu.VMEM((tm, D_MODEL), BF16), pltpu.VMEM((tm, D_MODEL), F32)]
    if seq_mode:
        scratch += [pltpu.VMEM((nf, SUBLANES, fc), F32), pltpu.VMEM((tm + SUBLANES, fc), F32)]
    return pl.pallas_call(
        functools.partial(_ffn_body, seq_mode=seq_mode, final=final, nseq_tiles=nseq_tiles, tm=tm, fc=fc),
        grid=(m // tm, nf), in_specs=in_specs, out_specs=out_specs, out_shape=out_shape,
        scratch_shapes=scratch, compiler_params=_cparams("arbitrary", "arbitrary"),
        name="ffn_seq" if seq_mode else "ffn_tok",
    )(*args)


def _rope_tables(pos, half, period, lane_off):
    inv = jnp.exp(-math.log(ROPE_THETA) * jnp.arange(half, dtype=F32) / half)
    ang = pos.astype(F32)[:, None] * inv[None, :]
    cos, sin = jnp.cos(ang), jnp.sin(ang)
    n = pos.shape[0]
    one = jnp.ones((n, lane_off), F32)
    zero = jnp.zeros((n, lane_off), F32)
    padw = period - lane_off - 2 * half
    pad1, pad0 = jnp.ones((n, padw), F32), jnp.zeros((n, padw), F32)
    z = jnp.zeros_like(sin)
    cos_t = jnp.concatenate([one, cos, cos, pad1], axis=1)
    sin_a = jnp.concatenate([zero, -sin, z, pad0], axis=1)
    sin_b = jnp.concatenate([zero, z, sin, pad0], axis=1)
    rep = LANES // period
    return tuple(jnp.tile(t, (1, rep)) for t in (cos_t, sin_a, sin_b))


def _mla_weights(w_in, g_q, g_kv, w_uq, w_uk, w_uv):
    kr_cols = w_in[:, MLA_Q_RANK + MLA_KV_RANK:]
    slab = jnp.concatenate([jnp.zeros((D_MODEL, MLA_NOPE), F32), kr_cols,
                            jnp.zeros((D_MODEL, LANES - MLA_QK), F32)], axis=1)
    w_in_p = jnp.concatenate([w_in[:, :MLA_Q_RANK + MLA_KV_RANK], slab], axis=1)
    uq = w_uq.reshape(MLA_Q_RANK, N_HEADS, MLA_QK)
    uq = jnp.pad(uq, ((0, 0), (0, 0), (0, LANES - MLA_QK))).reshape(MLA_Q_RANK, N_HEADS * LANES)
    uk = jnp.pad(w_uk, ((0, 0), (0, 0), (0, LANES - MLA_NOPE))).reshape(MLA_KV_RANK, N_HEADS * LANES)
    return dict(
        w_in=w_in_p.astype(BF16), g_q=g_q.reshape(1, MLA_Q_RANK), g_kv=g_kv.reshape(1, MLA_KV_RANK),
        w_uq=uq.astype(BF16), w_uk=uk.astype(BF16),
        w_uv=w_uv.reshape(MLA_KV_RANK, N_HEADS * MLA_V).astype(BF16),
        w_ukt=jnp.transpose(w_uk, (1, 2, 0)).astype(BF16),
        w_uvh=jnp.transpose(w_uv, (1, 0, 2)).astype(BF16),
    )


def _repeat_q(q_hm, n_rep):
    q = jnp.transpose(q_hm, (1, 0, 2)).astype(F32)
    return jnp.tile(q, (1, 1, n_rep))


def _kv_pages_t(cache):
    p, s = cache.shape[0], cache.shape[1]
    return jnp.transpose(cache, (0, 2, 3, 1)).reshape(p, KV_W, s)


def kernel(x_prompt, x_sample, cache_sb_k, cache_sb_v, cache_mla_ckv, cache_mla_kr, cache_fox_k, cache_fox_v, cache_fox_logf, state_swa_k, state_swa_v, cache_mem_k, cache_mem_v, state_ffn_conv, page_table, mem_prompt, g_mix, w_sb_qkv, w_sb_o, w_mla_in, g_mla_q, g_mla_kv, w_mla_uq, w_mla_uk, w_mla_uv, w_mla_o, w_fox_in, b_fox_f, w_fox_o, w_swa_qkv, swa_sinks, w_swa_o, g_cross, g_mem, w_cq, w_ckv, w_co, g_ffn, w_ffn_in, w_ffn_conv, b_ffn_conv, w_ffn_out, g_final):
    bp, seq, _ = x_prompt.shape
    bs = x_sample.shape[0]
    n_pages = page_table.shape[1]
    past = n_pages * PAGE_SIZE
    depth = g_mix.shape[0]
    n_mem = mem_prompt.shape[1]
    mp = bp * seq
    win = state_swa_k.shape[2]
    pb = math.gcd(n_pages, 32)

    bf = lambda a: a.astype(BF16)
    pos_p = jnp.arange(seq)
    pos_s = jnp.full((bs,), past, jnp.int32)
    swa_tabs_p = _rope_tables(pos_p, HEAD_DIM // 2, HEAD_DIM, 0)
    swa_tabs_s = _rope_tables(pos_s, HEAD_DIM // 2, HEAD_DIM, 0)
    mla_tabs_p = _rope_tables(pos_p, MLA_ROPE // 2, LANES, MLA_NOPE)
    mla_tabs_s = _rope_tables(pos_s, MLA_ROPE // 2, LANES, MLA_NOPE)
    ident_pages = jnp.arange(bs, dtype=jnp.int32).reshape(bs, 1)
    mem_k_rows = cache_mem_k.reshape(depth * bs, n_mem * MEM_HEADS, MEM_HEAD_DIM)
    mem_v_rows = cache_mem_v.reshape(depth * bs, n_mem * MEM_HEADS, MEM_HEAD_DIM)

    outs = {}
    xp = x_prompt.reshape(mp, D_MODEL)
    xs = x_sample.reshape(bs, D_MODEL)
    mem_flat = mem_prompt.reshape(bp * n_mem, D_MODEL)
    mem_k_p, mem_v_p, conv_p, conv_s = [], [], [], []
    y_p = y_s = None

    for i in range(depth):
        m, j = i % 4, i // 4
        gm = g_mix[i]
        if m == 0:
            w = bf(w_sb_qkv[j])
            q, k, v, kh, vh = _proj_qkv(xp, gm, w)
            o_p = _attn_sb(q, kh, vh, bp, seq)
            outs.setdefault("sb_k_p", []).append(k.reshape(bp, seq, N_KV, HEAD_DIM))
            outs.setdefault("sb_v_p", []).append(v.reshape(bp, seq, N_KV, HEAD_DIM))
            q, k, v, kh, vh = _proj_qkv(xs, gm, w)
            o_s = _decode("sb", _repeat_q(q, N_KV), _kv_pages_t(cache_sb_k[j]), _kv_pages_t(cache_sb_v[j]),
                          page_table, pb=pb)
            o_s = o_s.reshape(bs, Q_W)
            outs.setdefault("sb_k_s", []).append(k.reshape(bs, 1, N_KV, HEAD_DIM))
            outs.setdefault("sb_v_s", []).append(v.reshape(bs, 1, N_KV, HEAD_DIM))
            w_o = bf(w_sb_o[j])
        elif m == 1:
            wp = _mla_weights(w_mla_in[j], g_mla_q[j], g_mla_kv[j], w_mla_uq[j], w_mla_uk[j], w_mla_uv[j])
            q, kh, vh, ckv, kr = _proj_mla(xp, gm, wp, mla_tabs_p, absorbed=False)
            o_p = _attn("mla", q, kh, vh, bp, seq)
            outs.setdefault("mla_ckv_p", []).append(ckv.reshape(bp, seq, MLA_KV_RANK))
            outs.setdefault("mla_kr_p", []).append(kr[:, MLA_NOPE:MLA_QK].reshape(bp, seq, MLA_ROPE))
            q, kh, vh, ckv, kr, qlat = _proj_mla(xs, gm, wp, mla_tabs_s, absorbed=True)
            kr32 = kr[:, MLA_NOPE:MLA_QK]
            qr = jnp.transpose(q[:, :, MLA_NOPE:MLA_QK], (1, 0, 2)).astype(F32)
            o_lat = _decode("mla", jnp.transpose(qlat, (1, 0, 2)), cache_mla_ckv[j], None, page_table,
                            pb=pb, qr=qr, kr_pool=jnp.transpose(cache_mla_kr[j], (0, 2, 1)),
                            k_new=ckv.reshape(bs, 1, MLA_KV_RANK), kr_new=kr32.reshape(bs, 1, MLA_ROPE))
            o_s = _heads_matmul(jnp.transpose(o_lat, (1, 0, 2)), wp["w_uvh"])
            o_s = jnp.transpose(o_s, (1, 0, 2)).reshape(bs, Q_W)
            outs.setdefault("mla_ckv_s", []).append(ckv.reshape(bs, 1, MLA_KV_RANK))
            outs.setdefault("mla_kr_s", []).append(kr32.reshape(bs, 1, MLA_ROPE))
            w_o = bf(w_mla_o[j])
        elif m == 2:
            w = bf(w_fox_in[j])
            q, k, v, kh, vh, lf, c = _proj_qkv(xp, gm, w, b_f=b_fox_f[j], seq_len=seq)
            cq = jnp.transpose(c.reshape(mp, N_KV, GROUP), (1, 0, 2))
            ck = jnp.transpose(c.reshape(mp, N_KV, GROUP), (1, 2, 0))
            o_p = _attn("fox", q, kh, vh, bp, seq, cq=cq, ck=ck)
            outs.setdefault("fox_k_p", []).append(k.reshape(bp, seq, N_KV, HEAD_DIM))
            outs.setdefault("fox_v_p", []).append(v.reshape(bp, seq, N_KV, HEAD_DIM))
            outs.setdefault("fox_lf_p", []).append(lf.reshape(bp, seq, N_HEADS))
            q, k, v, kh, vh, lf, c = _proj_qkv(xs, gm, w, b_f=b_fox_f[j], seq_len=1)
            o_s = _decode("fox", _repeat_q(q, N_KV), _kv_pages_t(cache_fox_k[j]), _kv_pages_t(cache_fox_v[j]),
                          page_table, pb=pb, lf_pool=jnp.transpose(cache_fox_logf[j], (0, 2, 1)),
                          k_new=k.reshape(bs, 1, KV_W), v_new=v.reshape(bs, 1, KV_W),
                          lf_new=lf.reshape(bs, N_HEADS, 1))
            o_s = o_s.reshape(bs, Q_W)
            outs.setdefault("fox_k_s", []).append(k.reshape(bs, 1, N_KV, HEAD_DIM))
            outs.setdefault("fox_v_s", []).append(v.reshape(bs, 1, N_KV, HEAD_DIM))
            outs.setdefault("fox_lf_s", []).append(lf.reshape(bs, 1, N_HEADS))
            w_o = bf(w_fox_o[j])
        else:
            w = bf(w_swa_qkv[j])
            q, k, v, kh, vh = _proj_qkv(xp, gm, w, rope_tabs=swa_tabs_p)
            o_p = _attn("swa", q, kh, vh, bp, seq, sinks=swa_sinks[j])
            outs.setdefault("swa_k_p", []).append(k.reshape(bp, seq, N_KV, HEAD_DIM)[:, seq - win:])
            outs.setdefault("swa_v_p", []).append(v.reshape(bp, seq, N_KV, HEAD_DIM)[:, seq - win:])
            q, k, v, kh, vh = _proj_qkv(xs, gm, w, rope_tabs=swa_tabs_s)
            o_s = _decode("swa", _repeat_q(q, N_KV), _kv_pages_t(state_swa_k[j]), _kv_pages_t(state_swa_v[j]),
                          ident_pages, pb=1, k_new=k.reshape(bs, 1, KV_W), v_new=v.reshape(bs, 1, KV_W),
                          sinks=swa_sinks[j])
            o_s = o_s.reshape(bs, Q_W)
            k_all = jnp.concatenate([state_swa_k[j], k.reshape(bs, 1, N_KV, HEAD_DIM)], axis=1)
            v_all = jnp.concatenate([state_swa_v[j], v.reshape(bs, 1, N_KV, HEAD_DIM)], axis=1)
            outs.setdefault("swa_k_s", []).append(k_all[:, -win:])
            outs.setdefault("swa_v_s", []).append(v_all[:, -win:])
            w_o = bf(w_swa_o[j])

        kv_f32, kv_b16 = _proj_plain(mem_flat, g_mem[i], bf(w_ckv[i]))
        mem_k_p.append(kv_f32[:, :MEM_W].reshape(bp, n_mem, MEM_HEADS, MEM_HEAD_DIM))
        mem_v_p.append(kv_f32[:, MEM_W:].reshape(bp, n_mem, MEM_HEADS, MEM_HEAD_DIM))
        w_cq_b, w_co_b = bf(w_cq[i]), bf(w_co[i])
        x1_p, oc_p = _oproj(xp, o_p, w_o, g_cross[i], w_cq_b,
                            mem=(kv_b16[:, :MEM_W], kv_b16[:, MEM_W:]), seq_len=seq)
        x1_s, qc_s = _oproj(xs, o_s, w_o, g_cross[i], w_cq_b)
        qc_rows = jnp.pad(qc_s.reshape(bs, MEM_HEADS, MEM_HEAD_DIM), ((0, 0), (0, SUBLANES - MEM_HEADS), (0, 0)))
        oc_s = _cross_decode(qc_rows, mem_k_rows, mem_v_rows, i)
        oc_s = oc_s[:, :MEM_HEADS].reshape(bs, MEM_W)

        fin = g_final if i == depth - 1 else None
        w_in_b, w_out_b = bf(w_ffn_in[i]), bf(w_ffn_out[i])
        res = _ffn(x1_p, oc_p, w_co_b, g_ffn[i], w_in_b, w_ffn_conv[i], b_ffn_conv[i], w_out_b,
                   seq_len=seq, g_final=fin)
        xp, tails = res[0], res[1]
        if fin is not None:
            y_p = res[2]
        tps = tails.shape[0] // bp
        conv_p.append(tails.reshape(bp, tps, SUBLANES, D_FF)[:, -1, SUBLANES - (CONV_W - 1):])
        buf = state_ffn_conv[i]
        res = _ffn(x1_s, oc_s, w_co_b, g_ffn[i], w_in_b, w_ffn_conv[i], b_ffn_conv[i], w_out_b,
                   buf=(buf[:, 0], buf[:, 1]), g_final=fin)
        xs, gate_s = res[0], res[1]
        if fin is not None:
            y_s = res[2]
        conv_s.append(jnp.stack([buf[:, 1], gate_s], axis=1))

    stk = lambda name: jnp.stack(outs[name])
    return (y_p.reshape(bp, seq, D_MODEL), y_s.reshape(bs, 1, D_MODEL),
            stk("sb_k_p"), stk("sb_v_p"), stk("sb_k_s"), stk("sb_v_s"),
            stk("mla_ckv_p"), stk("mla_kr_p"), stk("mla_ckv_s"), stk("mla_kr_s"),
            stk("fox_k_p"), stk("fox_v_p"), stk("fox_lf_p"),
            stk("fox_k_s"), stk("fox_v_s"), stk("fox_lf_s"),
            stk("swa_k_p"), stk("swa_v_p"), stk("swa_k_s"), stk("swa_v_s"),
            jnp.stack(mem_k_p), jnp.stack(mem_v_p),
            jnp.stack(conv_p), jnp.stack(conv_s))
```

```python
import functools
import math

import jax
import jax.numpy as jnp
from jax import lax
from jax.experimental import pallas as pl
from jax.experimental.pallas import tpu as pltpu

F32 = jnp.float32
BF16 = jnp.bfloat16

D_MODEL = 1024
N_HEADS = 16
HEAD_DIM = 64
N_KV = 4
GROUP = N_HEADS // N_KV
Q_W = N_HEADS * HEAD_DIM
KV_W = N_KV * HEAD_DIM
QKV_W = Q_W + 2 * KV_W
MLA_Q_RANK = 768
MLA_KV_RANK = 256
MLA_NOPE = 64
MLA_ROPE = 32
MLA_V = 64
MLA_QK = MLA_NOPE + MLA_ROPE
WINDOW = 128
MEM_HEADS = 4
MEM_HEAD_DIM = 128
MEM_W = MEM_HEADS * MEM_HEAD_DIM
D_FF = 2816
CONV_W = 3
ROPE_THETA = 10000.0
EPS = 1e-6
PAGE_SIZE = 128

LANES = 128
SUBLANES = 8
NEG_BIG = -1e30
VMEM_LIMIT = 56 * 1024 * 1024
CUM_CHUNK = 256
LOG2E = math.log2(math.e)
V_ONE = HEAD_DIM


def _cparams(*sem):
    return pltpu.CompilerParams(dimension_semantics=sem, vmem_limit_bytes=VMEM_LIMIT)


def _full(shape):
    return pl.BlockSpec(shape, lambda *_: (0,) * len(shape))


def _row_tile(m, pref):
    t = min(m, pref)
    assert m % t == 0, (m, t)
    return t


def _rms(x, g):
    return x * lax.rsqrt(jnp.mean(x * x, axis=-1, keepdims=True) + EPS) * g


def _dot(a, b):
    return jnp.dot(a, b, preferred_element_type=F32)


def _dot_nt(a, b):
    return lax.dot_general(a, b, (((1,), (1,)), ((), ())), preferred_element_type=F32)


def _log_sigmoid(z):
    return jnp.minimum(z, 0.0) - jnp.log1p(jnp.exp(-jnp.abs(z)))


def _log2_one_minus_sigmoid(z2):
    nz = -z2
    return jnp.minimum(nz, 0.0) - jnp.log2(1.0 + jnp.exp2(jnp.minimum(z2, nz)))


def _rope_slab(x, cos_t, sin_a, sin_b, half):
    return (x * cos_t + pltpu.roll(x, LANES - half, axis=1) * sin_a
            + pltpu.roll(x, half, axis=1) * sin_b)


def _rope_wide(x, cos_t, sin_a, sin_b, half):
    n = x.shape[1] // LANES
    return jnp.concatenate(
        [_rope_slab(x[:, i * LANES:(i + 1) * LANES], cos_t, sin_a, sin_b, half) for i in range(n)],
        axis=1)


def _split2(x):
    a = x.astype(BF16)
    b = (x - a.astype(F32)).astype(BF16)
    return a, b


def _split3(x):
    a = x.astype(BF16)
    r = x - a.astype(F32)
    b = r.astype(BF16)
    c = (r - b.astype(F32)).astype(BF16)
    return a, b, c


def _store_heads(dst_ref, a, n_heads, scale=None):
    for h in range(n_heads):
        blk = a[:, h * HEAD_DIM:(h + 1) * HEAD_DIM]
        if scale is not None:
            blk = blk * scale
        dst_ref[h] = blk.astype(dst_ref.dtype)


def _store_v_heads(dst_ref, v, n_heads):
    rows = v.shape[0]
    one_col = (lax.broadcasted_iota(jnp.int32, (rows, LANES - HEAD_DIM), 1) == 0).astype(F32)
    for h in range(n_heads):
        blk = jnp.concatenate([v[:, h * HEAD_DIM:(h + 1) * HEAD_DIM], one_col], axis=1)
        dst_ref[h] = blk.astype(dst_ref.dtype)


def _proj_qkv_body(*refs, rope, fox, nseq_tiles):
    it = iter(refs)
    x_ref, g_ref, w_ref = next(it), next(it), next(it)
    if rope:
        cos_ref, sa_ref, sb_ref = next(it), next(it), next(it)
    if fox:
        bf_ref, tri_ref = next(it), next(it)
    q_ref, k_ref, v_ref, kh_ref, vh_ref = next(it), next(it), next(it), next(it), next(it)
    if fox:
        lf_ref, c_ref, carry_sc = next(it), next(it), next(it)

    h = _rms(x_ref[...], g_ref[...]).astype(BF16)
    a = _dot(h, w_ref[...])
    q = a[:, :Q_W]
    k = a[:, Q_W:Q_W + KV_W]
    v = a[:, Q_W + KV_W:QKV_W]
    if rope:
        cos_t, sin_a, sin_b = cos_ref[...], sa_ref[...], sb_ref[...]
        q = _rope_wide(q, cos_t, sin_a, sin_b, HEAD_DIM // 2)
        k = _rope_wide(k, cos_t, sin_a, sin_b, HEAD_DIM // 2)
    k_ref[...] = k
    v_ref[...] = v
    _store_heads(q_ref, q, N_HEADS, scale=HEAD_DIM ** -0.5 * LOG2E)
    _store_heads(kh_ref, k, N_KV)
    _store_v_heads(vh_ref, v, N_KV)
    if fox:
        @pl.when(pl.program_id(0) % nseq_tiles == 0)
        def _():
            carry_sc[...] = jnp.zeros_like(carry_sc)

        lf = _log_sigmoid(a[:, QKV_W:QKV_W + N_HEADS] + bf_ref[...])
        lf_ref[...] = lf
        tri = tri_ref[...]
        t3 = _split3(lf)
        cs = _dot(tri, t3[0]) + _dot(tri, t3[1]) + _dot(tri, t3[2]) + carry_sc[...]
        c_ref[...] = cs * LOG2E
        carry_sc[...] = cs[cs.shape[0] - 1:, :]


def _proj_qkv(x, g, w, *, rope_tabs=None, b_f=None, seq_len=1):
    m = x.shape[0]
    rope = rope_tabs is not None
    fox = b_f is not None
    tm = _row_tile(min(m, seq_len) if fox and seq_len > 1 else m, 512)
    nseq_tiles = max(seq_len // tm, 1)
    in_specs = [pl.BlockSpec((tm, D_MODEL), lambda i: (i, 0)), _full((1, D_MODEL)), _full(w.shape)]
    args = [x, g.reshape(1, D_MODEL), w]
    if rope:
        nt = rope_tabs[0].shape[0] // tm
        for t in rope_tabs:
            in_specs.append(pl.BlockSpec((tm, LANES), lambda i, nt=nt: (i % nt, 0)))
            args.append(t)
    if fox:
        in_specs += [_full((1, N_HEADS)), _full((tm, tm))]
        args += [b_f.reshape(1, N_HEADS), jnp.tril(jnp.ones((tm, tm), F32)).astype(BF16)]
    out_shape = [
        jax.ShapeDtypeStruct((N_HEADS, m, HEAD_DIM), BF16),
        jax.ShapeDtypeStruct((m, KV_W), F32),
        jax.ShapeDtypeStruct((m, KV_W), F32),
        jax.ShapeDtypeStruct((N_KV, m, HEAD_DIM), BF16),
        jax.ShapeDtypeStruct((N_KV, m, LANES), BF16),
    ]
    out_specs = [
        pl.BlockSpec((N_HEADS, tm, HEAD_DIM), lambda i: (0, i, 0)),
        pl.BlockSpec((tm, KV_W), lambda i: (i, 0)),
        pl.BlockSpec((tm, KV_W), lambda i: (i, 0)),
        pl.BlockSpec((N_KV, tm, HEAD_DIM), lambda i: (0, i, 0)),
        pl.BlockSpec((N_KV, tm, LANES), lambda i: (0, i, 0)),
    ]
    scratch = []
    if fox:
        out_shape += [jax.ShapeDtypeStruct((m, N_HEADS), F32)] * 2
        out_specs += [pl.BlockSpec((tm, N_HEADS), lambda i: (i, 0))] * 2
        scratch = [pltpu.VMEM((1, N_HEADS), F32)]
    return pl.pallas_call(
        functools.partial(_proj_qkv_body, rope=rope, fox=fox, nseq_tiles=nseq_tiles),
        grid=(m // tm,), in_specs=in_specs, out_specs=out_specs, out_shape=out_shape,
        scratch_shapes=scratch, compiler_params=_cparams("arbitrary"),
        name="proj_fox" if fox else ("proj_swa" if rope else "proj_sb"),
    )(*args)


def _proj_mla_body(x_ref, g_ref, win_ref, gq_ref, gkv_ref, wuq_ref, wuk_ref, wuv_ref, wukt_ref,
                   cos_ref, sa_ref, sb_ref,
                   q_ref, kh_ref, vh_ref, ckv_ref, kr_ref, *maybe_qlat, absorbed):
    scale = MLA_QK ** -0.5 * LOG2E
    half = MLA_ROPE // 2
    cos_t, sin_a, sin_b = cos_ref[...], sa_ref[...], sb_ref[...]
    h = _rms(x_ref[...], g_ref[...]).astype(BF16)
    a = _dot(h, win_ref[...])
    cq = _rms(a[:, :MLA_Q_RANK], gq_ref[...]).astype(BF16)
    ckv = _rms(a[:, MLA_Q_RANK:MLA_Q_RANK + MLA_KV_RANK], gkv_ref[...])
    ckv_ref[...] = ckv
    ckv_b = ckv.astype(BF16)
    kr = _rope_slab(a[:, MLA_Q_RANK + MLA_KV_RANK:], cos_t, sin_a, sin_b, half)
    kr_ref[...] = kr
    qf = _dot(cq, wuq_ref[...])
    kn = _dot(ckv_b, wuk_ref[...])
    _store_v_heads(vh_ref, _dot(ckv_b, wuv_ref[...]), N_HEADS)
    for hd in range(N_HEADS):
        sl = slice(hd * LANES, (hd + 1) * LANES)
        qh = _rope_slab(qf[:, sl], cos_t, sin_a, sin_b, half) * scale
        q_ref[hd] = qh.astype(BF16)
        kh_ref[hd] = (kn[:, sl] + kr).astype(BF16)
        if absorbed:
            qn = qh[:, :MLA_NOPE].astype(BF16)
            maybe_qlat[0][hd] = _dot(qn, wukt_ref[hd])


def _proj_mla(x, g, wp, rope_tabs, *, absorbed):
    m = x.shape[0]
    tm = _row_tile(m, 512)
    nt = rope_tabs[0].shape[0] // tm
    args = [x, g.reshape(1, D_MODEL), wp["w_in"], wp["g_q"], wp["g_kv"], wp["w_uq"], wp["w_uk"],
            wp["w_uv"], wp["w_ukt"]]
    in_specs = [pl.BlockSpec((tm, D_MODEL), lambda i: (i, 0))] + [_full(a.shape) for a in args[1:]]
    for t in rope_tabs:
        in_specs.append(pl.BlockSpec((tm, LANES), lambda i, nt=nt: (i % nt, 0)))
        args.append(t)
    out_shape = [
        jax.ShapeDtypeStruct((N_HEADS, m, LANES), BF16),
        jax.ShapeDtypeStruct((N_HEADS, m, LANES), BF16),
        jax.ShapeDtypeStruct((N_HEADS, m, LANES), BF16),
        jax.ShapeDtypeStruct((m, MLA_KV_RANK), F32),
        jax.ShapeDtypeStruct((m, LANES), F32),
    ]
    hm = pl.BlockSpec((N_HEADS, tm, LANES), lambda i: (0, i, 0))
    out_specs = [hm, hm, hm,
                 pl.BlockSpec((tm, MLA_KV_RANK), lambda i: (i, 0)),
                 pl.BlockSpec((tm, LANES), lambda i: (i, 0))]
    if absorbed:
        out_shape.append(jax.ShapeDtypeStruct((N_HEADS, m, MLA_KV_RANK), F32))
        out_specs.append(pl.BlockSpec((N_HEADS, tm, MLA_KV_RANK), lambda i: (0, i, 0)))
    return pl.pallas_call(
        functools.partial(_proj_mla_body, absorbed=absorbed),
        grid=(m // tm,), in_specs=in_specs, out_specs=out_specs, out_shape=out_shape,
        compiler_params=_cparams("arbitrary"), name="proj_mla",
    )(*args)


def _proj_plain_body(x_ref, g_ref, w_ref, o_ref, ob_ref):
    a = _dot(_rms(x_ref[...], g_ref[...]).astype(BF16), w_ref[...])
    o_ref[...] = a
    ob_ref[...] = a.astype(BF16)


def _proj_plain(x, g, w):
    m, n = x.shape[0], w.shape[1]
    tm = _row_tile(m, 512)
    return pl.pallas_call(
        _proj_plain_body, grid=(m // tm,),
        in_specs=[pl.BlockSpec((tm, D_MODEL), lambda i: (i, 0)), _full((1, D_MODEL)), _full(w.shape)],
        out_specs=[pl.BlockSpec((tm, n), lambda i: (i, 0))] * 2,
        out_shape=[jax.ShapeDtypeStruct((m, n), F32), jax.ShapeDtypeStruct((m, n), BF16)],
        compiler_params=_cparams("arbitrary"), name="proj_mem",
    )(x, g.reshape(1, D_MODEL), w)


FLAG_FIRST, FLAG_LAST, FLAG_MASK = 1, 2, 4


def _schedule(n_tiles, tile, *, window=None, descending=False, strict=False):
    qi, kj, fl = [], [], []
    for i in range(n_tiles):
        q_lo = i * tile
        lo = 0 if window is None else max(0, (q_lo - window) // tile)
        js = list(range(lo, i + 1))
        if descending:
            js = js[::-1]
        for n, j in enumerate(js):
            k_hi = (j + 1) * tile - 1
            need_mask = (k_hi >= q_lo) if strict else (k_hi > q_lo)
            f = (FLAG_FIRST if n == 0 else 0) | (FLAG_LAST if n == len(js) - 1 else 0)
            f |= FLAG_MASK if need_mask else 0
            qi.append(i), kj.append(j), fl.append(f)
    return (jnp.asarray(qi, jnp.int32), jnp.asarray(kj, jnp.int32), jnp.asarray(fl, jnp.int32))


def _attn_body(qi_ref, kj_ref, fl_ref, *refs, mode, nh, tq, tk):
    it = iter(refs)
    if mode == "swa":
        sink_ref = next(it)
    q_ref, k_ref, v_ref = next(it), next(it), next(it)
    if mode == "fox":
        cq_ref, ck_ref = next(it), next(it)
    o_ref, m_sc, acc_sc = next(it), next(it), next(it)
    if mode == "fox":
        cqr_sc = next(it)
    g = pl.program_id(1)
    s_id = pl.program_id(2)
    flag = fl_ref[s_id]
    q_lo = qi_ref[s_id] * tq
    k_lo = kj_ref[s_id] * tk
    own_kv = mode == "mla"

    @pl.when((flag & FLAG_FIRST) != 0)
    def _():
        if mode == "swa":
            one_col = (lax.broadcasted_iota(jnp.int32, (tq, LANES), 1) == V_ONE).astype(F32)
            for hh in range(nh):
                m_sc[hh] = jnp.full((tq, LANES), sink_ref[g * nh + hh] * LOG2E, F32)
                acc_sc[hh] = one_col
        else:
            m_sc[...] = jnp.full(m_sc.shape, NEG_BIG, F32)
            acc_sc[...] = jnp.zeros_like(acc_sc)
        if mode == "fox":
            for hh in range(nh):
                cqr_sc[hh] = jnp.broadcast_to(cq_ref[0][:, hh:hh + 1], (tq, LANES))

    def run_blocks(blocks):
        for row0, nrows, col0, ncols, masked in blocks:
            rs, cs = slice(row0, row0 + nrows), slice(col0, col0 + ncols)
            if masked:
                qpos = q_lo + row0 + lax.broadcasted_iota(jnp.int32, (nrows, ncols), 0)
                kpos = k_lo + col0 + lax.broadcasted_iota(jnp.int32, (nrows, ncols), 1)
                vis = kpos <= qpos
                if mode == "swa":
                    vis = vis & (qpos - kpos <= WINDOW)
            for hh in range(nh):
                kv = hh if own_kv else 0
                s = _dot_nt(q_ref[hh, rs, :], k_ref[kv, cs, :])
                if mode == "fox":
                    s = s - ck_ref[0, hh:hh + 1, cs]
                if masked:
                    s = jnp.where(vis, s, NEG_BIG)
                m_prev = m_sc[hh, rs, :]
                smax = jnp.max(s, axis=-1, keepdims=True)
                if mode == "fox":
                    cq = cqr_sc[hh, rs, :]
                    m_new = jnp.maximum(m_prev, smax + cq)
                    shift = m_new - cq
                else:
                    m_new = jnp.maximum(m_prev, smax)
                    shift = m_new
                alpha = jnp.exp2(m_prev - m_new)
                p = jnp.exp2(s - jnp.concatenate([shift] * (ncols // LANES), axis=1))
                acc_sc[hh, rs, :] = alpha * acc_sc[hh, rs, :] + _dot(p.astype(BF16), v_ref[kv, cs, :])
                m_sc[hh, rs, :] = m_new

    if mode == "swa":
        w = WINDOW
        on_diag = [(c * w, min(2 * w, tq - c * w), c * w, w, True) for c in range(tk // w)]
        off_diag = [(0, w, tk - w, w, True)]
    else:
        h = tk // 2
        on_diag = [(0, tq, 0, h, True), (h, tq - h, h, tk - h, True)]
        off_diag = [(0, tq, 0, tk, False)]

    @pl.when((flag & FLAG_MASK) != 0)
    def _():
        run_blocks(on_diag)

    @pl.when((flag & FLAG_MASK) == 0)
    def _():
        run_blocks(off_diag)

    @pl.when((flag & FLAG_LAST) != 0)
    def _():
        for hh in range(nh):
            acc = acc_sc[hh]
            o = acc[:, :HEAD_DIM] / acc[:, V_ONE:V_ONE + 1]
            o_ref[:, hh * HEAD_DIM:(hh + 1) * HEAD_DIM] = o.astype(o_ref.dtype)


def _attn(mode, q_hm, k_hm, v_hm, batch, seq, *, cq=None, ck=None, sinks=None):
    nh = GROUP
    tq = tk = _row_tile(seq, 512 if mode == "swa" else 1024)
    nq = seq // tq
    dk = q_hm.shape[-1]
    nkv = nh if mode == "mla" else 1
    qi, kj, fl = _schedule(nq, tq, window=WINDOW if mode == "swa" else None)
    prefetch = [qi, kj, fl]
    if mode == "swa":
        prefetch.append(sinks.astype(F32))

    def qmap(b, g, s, qi, kj, *_):
        return (g, b * nq + qi[s], 0)

    def kmap(b, g, s, qi, kj, *_):
        return (g, b * nq + kj[s], 0)

    in_specs = [pl.BlockSpec((nh, tq, dk), qmap), pl.BlockSpec((nkv, tk, dk), kmap),
                pl.BlockSpec((nkv, tk, LANES), kmap)]
    args = [q_hm, k_hm, v_hm]
    if mode == "fox":
        in_specs.append(pl.BlockSpec((1, tq, GROUP), qmap))
        in_specs.append(pl.BlockSpec((1, GROUP, tk), lambda b, g, s, qi, kj, *_: (g, 0, b * nq + kj[s])))
        args += [cq, ck]
    grid_spec = pltpu.PrefetchScalarGridSpec(
        num_scalar_prefetch=len(prefetch), grid=(batch, N_HEADS // nh, int(qi.shape[0])),
        in_specs=in_specs,
        out_specs=pl.BlockSpec((tq, nh * HEAD_DIM), lambda b, g, s, qi, kj, *_: (b * nq + qi[s], g)),
        scratch_shapes=[pltpu.VMEM((nh, tq, LANES), F32)] * (3 if mode == "fox" else 2))
    return pl.pallas_call(
        functools.partial(_attn_body, mode=mode, nh=nh, tq=tq, tk=tk),
        grid_spec=grid_spec, out_shape=jax.ShapeDtypeStruct((batch * seq, Q_W), BF16),
        compiler_params=_cparams("arbitrary", "arbitrary", "arbitrary"), name="attn_" + mode,
    )(*prefetch, *args)


def _attn_sb_body(qi_ref, kj_ref, fl_ref, q_ref, k_ref, v_ref, u_ref, o_ref, run_sc, acc_sc, *, nh, tq, tk):
    s_id = pl.program_id(2)
    flag = fl_ref[s_id]
    q_lo = qi_ref[s_id] * tq
    k_lo = kj_ref[s_id] * tk
    ck = min(CUM_CHUNK, tk)
    n_chunks = tk // ck

    @pl.when((flag & FLAG_FIRST) != 0)
    def _():
        run_sc[...] = jnp.zeros_like(run_sc)
        acc_sc[...] = jnp.zeros_like(acc_sc)

    def step(masked):
        u = u_ref[...]
        for r in reversed(range(n_chunks)):
            kc = k_ref[0, r * ck:(r + 1) * ck, :]
            vc = v_ref[0, r * ck:(r + 1) * ck, :]
            row0 = r * ck if masked else 0
            rows = tq - row0
            if masked:
                qpos = q_lo + row0 + lax.broadcasted_iota(jnp.int32, (rows, ck), 0)
                kpos = k_lo + r * ck + lax.broadcasted_iota(jnp.int32, (rows, ck), 1)
                vis = kpos < qpos
            for hh in range(nh):
                z = _dot_nt(q_ref[hh, row0:, :], kc)
                lr = _log2_one_minus_sigmoid(z)
                if masked:
                    lr = jnp.where(vis, lr, 0.0)
                incl = _dot(jnp.concatenate(_split2(lr), axis=1), u)
                run = run_sc[hh, row0:, :]
                w = jnp.exp2(z + incl + jnp.concatenate([run] * (ck // LANES), axis=1))
                if masked:
                    w = jnp.where(vis, w, 0.0)
                acc_sc[hh, row0:, :] = acc_sc[hh, row0:, :] + _dot(w.astype(BF16), vc)
                run_sc[hh, row0:, :] = run + jnp.broadcast_to(incl[:, 0:1], (rows, LANES))

    @pl.when((flag & FLAG_MASK) != 0)
    def _():
        step(True)

    @pl.when((flag & FLAG_MASK) == 0)
    def _():
        step(False)

    @pl.when((flag & FLAG_LAST) != 0)
    def _():
        for hh in range(nh):
            o_ref[:, hh * HEAD_DIM:(hh + 1) * HEAD_DIM] = acc_sc[hh][:, :HEAD_DIM].astype(o_ref.dtype)


def _attn_sb(q_hm, k_hm, v_hm, batch, seq):
    nh = GROUP
    tq = tk = _row_tile(seq, 1024)
    nq = seq // tq
    ck = min(CUM_CHUNK, tk)
    qi, kj, fl = _schedule(nq, tq, descending=True, strict=True)
    u = (jnp.arange(ck)[:, None] >= jnp.arange(ck)[None, :]).astype(BF16)
    u = jnp.concatenate([u, u], axis=0)

    def qmap(b, g, s, qi, kj, fl):
        return (g, b * nq + qi[s], 0)

    def kmap(b, g, s, qi, kj, fl):
        return (g, b * nq + kj[s], 0)

    grid_spec = pltpu.PrefetchScalarGridSpec(
        num_scalar_prefetch=3, grid=(batch, N_KV, int(qi.shape[0])),
        in_specs=[pl.BlockSpec((nh, tq, HEAD_DIM), qmap), pl.BlockSpec((1, tk, HEAD_DIM), kmap),
                  pl.BlockSpec((1, tk, LANES), kmap), pl.BlockSpec((2 * ck, ck), lambda *_: (0, 0))],
        out_specs=pl.BlockSpec((tq, nh * HEAD_DIM), lambda b, g, s, qi, kj, fl: (b * nq + qi[s], g)),
        scratch_shapes=[pltpu.VMEM((nh, tq, LANES), F32), pltpu.VMEM((nh, tq, LANES), F32)])
    return pl.pallas_call(
        functools.partial(_attn_sb_body, nh=nh, tq=tq, tk=tk),
        grid_spec=grid_spec, out_shape=jax.ShapeDtypeStruct((batch * seq, Q_W), BF16),
        compiler_params=_cparams("arbitrary", "arbitrary", "arbitrary"), name="attn_sb",
    )(qi, kj, fl, q_hm, k_hm, v_hm, u)


def _suffix_sum_lanes(x):
    h, w = x.shape
    n = w // LANES
    lane = lax.broadcasted_iota(jnp.int32, (h, LANES), 1)
    slabs = []
    for i in range(n):
        y = x[:, i * LANES:(i + 1) * LANES]
        d = 1
        while d < LANES:
            y = y + jnp.where(lane < LANES - d, pltpu.roll(y, LANES - d, axis=1), 0.0)
            d *= 2
        slabs.append(y)
    tots = [jnp.broadcast_to(y[:, 0:1], (h, LANES)) for y in slabs]
    run = jnp.zeros((h, LANES), F32)
    out = [None] * n
    for i in reversed(range(n)):
        out[i] = slabs[i] + run
        run = run + tots[i]
    return (jnp.concatenate(out, axis=1) if n > 1 else out[0]), run


def _decode_body(pt_ref, *refs, mode, pb, hq, new_key):
    it = iter(refs)
    q_ref = next(it)
    qr_ref = next(it) if mode == "mla" else None
    n_pools = 3 if mode == "fox" else 2
    pools = [next(it) for _ in range(n_pools)]
    if new_key:
        kn_ref = next(it)
        krn_ref = next(it) if mode == "mla" else None
        vn_ref = kn_ref if mode == "mla" else next(it)
        lfn_ref = next(it) if mode == "fox" else None
    sink_ref = next(it) if mode == "swa" else None
    o_ref, m_sc, l_sc, acc_sc, car_sc = next(it), next(it), next(it), next(it), next(it)
    bufs = [next(it) for _ in range(n_pools)]
    sem = next(it)
    b = pl.program_id(0)
    t = pl.program_id(1)
    nt = pl.num_programs(1)
    n_pages = nt * pb
    step = b * nt + t
    slot = step % 2

    def page_copies(bb, tt, sl):
        out = []
        for r in range(pb):
            page = pt_ref[bb * n_pages + n_pages - 1 - (tt * pb + r)]
            for n in range(n_pools):
                out.append(pltpu.make_async_copy(pools[n].at[page], bufs[n].at[sl, r], sem.at[sl, n]))
        return out

    @pl.when(step == 0)
    def _():
        for c in page_copies(0, 0, 0):
            c.start()

    @pl.when(step + 1 < pl.num_programs(0) * nt)
    def _():
        last_t = t + 1 == nt
        for c in page_copies(jnp.where(last_t, b + 1, b), jnp.where(last_t, 0, t + 1), 1 - slot):
            c.start()

    for c in page_copies(b, t, slot):
        c.wait()

    k_refs = [bufs[0].at[slot, r] for r in range(pb)]
    kr_refs = [bufs[1].at[slot, r] for r in range(pb)] if mode == "mla" else None
    v_refs = k_refs if mode == "mla" else [bufs[1].at[slot, r] for r in range(pb)]
    lf_refs = [bufs[2].at[slot, r] for r in range(pb)] if mode == "fox" else None
    dk = q_ref.shape[-1]
    grouped = mode in ("sb", "fox", "swa")
    softmax = mode != "sb"

    if grouped:
        row_kv = lax.broadcasted_iota(jnp.int32, (hq, dk), 0) // GROUP
        lane_kv = lax.broadcasted_iota(jnp.int32, (hq, dk), 1) // HEAD_DIM
        q = jnp.where(row_kv == lane_kv, q_ref[0], 0.0)
    else:
        q = q_ref[0]
    qb = q.astype(BF16)

    @pl.when(t == 0)
    def _():
        if mode == "swa":
            m_sc[...] = jnp.broadcast_to(sink_ref[...] * LOG2E, m_sc.shape)
            l_sc[...] = jnp.ones_like(l_sc)
        else:
            m_sc[...] = jnp.full(m_sc.shape, NEG_BIG, F32)
            l_sc[...] = jnp.zeros_like(l_sc)
        acc_sc[...] = jnp.zeros_like(acc_sc)
        if mode == "fox" and new_key:
            car_sc[...] = jnp.broadcast_to(lfn_ref[0] * LOG2E, car_sc.shape)
        else:
            car_sc[...] = jnp.zeros_like(car_sc)

    order = list(reversed(range(pb)))
    s_parts = []
    for r in order:
        kp = k_refs[r][...].astype(BF16)
        if grouped:
            s = _dot(qb, kp)
        else:
            s = _dot_nt(qb, kp)
        if mode == "mla":
            s = s + _dot(qr_ref[0].astype(BF16), kr_refs[r][...].astype(BF16))
        s_parts.append(s)
    s = jnp.concatenate(s_parts, axis=1) if pb > 1 else s_parts[0]
    ps = s.shape[1] // pb
    nrep = s.shape[1] // LANES

    def wide(x):
        return jnp.concatenate([x] * nrep, axis=1) if nrep > 1 else x

    def pv(p):
        pb16 = p.astype(BF16)
        out = None
        for n, r in enumerate(order):
            blk = pb16[:, n * ps:(n + 1) * ps]
            vp = v_refs[r][...].astype(BF16)
            term = _dot_nt(blk, vp) if grouped else _dot(blk, vp)
            out = term if out is None else out + term
        return out

    if mode == "sb":
        lr = _log2_one_minus_sigmoid(s)
        incl, tot = _suffix_sum_lanes(lr)
        w = jnp.exp2(s + incl + wide(car_sc[...]))
        acc_sc[...] = acc_sc[...] + pv(w)
        car_sc[...] = car_sc[...] + tot
    else:
        if mode == "fox":
            lf = jnp.concatenate([lf_refs[r][...] for r in order], axis=1) if pb > 1 else lf_refs[0][...]
            lf = lf * LOG2E
            incl, tot = _suffix_sum_lanes(lf)
            s = s + (incl - lf) + wide(car_sc[...])
            car_sc[...] = car_sc[...] + tot
        m_prev = m_sc[...]
        m_new = jnp.maximum(m_prev, jnp.max(s, axis=-1, keepdims=True))
        alpha = jnp.exp2(m_prev - m_new)
        p = jnp.exp2(s - wide(m_new))
        l_sc[...] = alpha * l_sc[...] + jnp.sum(p, axis=-1, keepdims=True)
        nacc = acc_sc.shape[-1] // LANES
        alpha_w = jnp.concatenate([alpha] * nacc, axis=1) if nacc > 1 else alpha
        acc_sc[...] = alpha_w * acc_sc[...] + pv(p)
        m_sc[...] = m_new

    @pl.when(t == pl.num_programs(1) - 1)
    def _():
        acc = acc_sc[...]
        if softmax:
            m_prev, l_prev = m_sc[...][:, :1], l_sc[...][:, :1]
            if new_key:
                s_new = jnp.sum(q * kn_ref[0], axis=-1, keepdims=True)
                if mode == "mla":
                    s_new = s_new + jnp.sum(qr_ref[0] * krn_ref[0], axis=-1, keepdims=True)
                m_new = jnp.maximum(m_prev, s_new)
                alpha = jnp.exp2(m_prev - m_new)
                p_new = jnp.exp2(s_new - m_new)
                l_prev = alpha * l_prev + p_new
                acc = alpha * acc + p_new * vn_ref[0]
            acc = acc / l_prev
        if grouped:
            row_kv = lax.broadcasted_iota(jnp.int32, (hq, HEAD_DIM), 0) // GROUP
            out = jnp.zeros((hq, HEAD_DIM), F32)
            for kv in range(N_KV):
                out = out + jnp.where(row_kv == kv, acc[:, kv * HEAD_DIM:(kv + 1) * HEAD_DIM], 0.0)
            o_ref[0] = out
        else:
            o_ref[0] = acc


def _decode(mode, q, k_pool, v_pool, page_table, *, pb, qr=None, kr_pool=None,
            lf_pool=None, k_new=None, v_new=None, kr_new=None, lf_new=None, sinks=None):
    bsz, hq, dk = q.shape
    n_pages = page_table.shape[1]
    assert n_pages % pb == 0
    grouped = mode in ("sb", "fox", "swa")
    vblk = (v_pool if v_pool is not None else k_pool).shape[1:]
    dvt = vblk[0] if grouped else vblk[1]
    out_d = HEAD_DIM if grouped else dvt
    new_key = k_new is not None
    pt = page_table.reshape(-1).astype(jnp.int32)

    def seq_map(b, t, pt):
        return (b, 0, 0)

    in_specs = [pl.BlockSpec((1, hq, dk), seq_map)]
    args = [q]
    if mode == "mla":
        in_specs.append(pl.BlockSpec((1, hq, qr.shape[-1]), seq_map))
        args.append(qr)
    pools = [k_pool, kr_pool if mode == "mla" else v_pool] + ([lf_pool] if mode == "fox" else [])
    for p in pools:
        in_specs.append(pl.BlockSpec(memory_space=pl.ANY))
        args.append(p)
    if new_key:
        in_specs.append(pl.BlockSpec((1, 1, dk), seq_map))
        args.append(k_new)
        if mode == "mla":
            in_specs.append(pl.BlockSpec((1, 1, kr_new.shape[-1]), seq_map))
            args.append(kr_new)
        else:
            in_specs.append(pl.BlockSpec((1, 1, dvt), seq_map))
            args.append(v_new)
        if mode == "fox":
            in_specs.append(pl.BlockSpec((1, hq, 1), seq_map))
            args.append(lf_new)
    if mode == "swa":
        in_specs.append(pl.BlockSpec((hq, 1), lambda b, t, pt: (0, 0)))
        args.append(sinks.reshape(hq, 1).astype(F32))
    grid_spec = pltpu.PrefetchScalarGridSpec(
        num_scalar_prefetch=1, grid=(bsz, n_pages // pb), in_specs=in_specs,
        out_specs=pl.BlockSpec((1, hq, out_d), seq_map),
        scratch_shapes=[pltpu.VMEM((hq, LANES), F32), pltpu.VMEM((hq, LANES), F32),
                        pltpu.VMEM((hq, dvt), F32), pltpu.VMEM((hq, LANES), F32)]
        + [pltpu.VMEM((2, pb) + p.shape[1:], p.dtype) for p in pools]
        + [pltpu.SemaphoreType.DMA((2, len(pools)))])
    return pl.pallas_call(
        functools.partial(_decode_body, mode=mode, pb=pb, hq=hq, new_key=new_key),
        grid_spec=grid_spec, out_shape=jax.ShapeDtypeStruct((bsz, hq, out_d), F32),
        compiler_params=_cparams("arbitrary", "arbitrary"), name="decode_" + mode,
    )(pt, *args)


def _cross_decode_body(q_ref, k_ref, v_ref, o_ref, *, nseq):
    rows = k_ref.shape[1]
    col_head = lax.broadcasted_iota(jnp.int32, (SUBLANES, rows), 1) % MEM_HEADS
    own = col_head == lax.broadcasted_iota(jnp.int32, (SUBLANES, rows), 0)
    for n in range(nseq):
        s = _dot_nt(q_ref[n].astype(BF16), k_ref[n].astype(BF16))
        s = jnp.where(own, s, NEG_BIG)
        p = jnp.exp2(s - jnp.max(s, axis=-1, keepdims=True))
        o = _dot(p.astype(BF16), v_ref[n].astype(BF16))
        o_ref[n] = o / jnp.sum(p, axis=-1, keepdims=True)


def _cross_decode(q, k_rows, v_rows, layer):
    bsz = q.shape[0]
    nseq = math.gcd(bsz, 8)
    steps = bsz // nseq
    blk = (nseq,) + k_rows.shape[1:]
    return pl.pallas_call(
        functools.partial(_cross_decode_body, nseq=nseq), grid=(steps,),
        in_specs=[pl.BlockSpec((nseq, SUBLANES, MEM_HEAD_DIM), lambda i: (i, 0, 0)),
                  pl.BlockSpec(blk, lambda i: (layer * steps + i, 0, 0)),
                  pl.BlockSpec(blk, lambda i: (layer * steps + i, 0, 0))],
        out_specs=pl.BlockSpec((nseq, SUBLANES, MEM_HEAD_DIM), lambda i: (i, 0, 0)),
        out_shape=jax.ShapeDtypeStruct((bsz, SUBLANES, MEM_HEAD_DIM), F32),
        compiler_params=_cparams("arbitrary"), name="decode_cross",
    )(q, k_rows, v_rows)


def _heads_matmul_body(x_ref, w_ref, o_ref):
    o_ref[0] = _dot(x_ref[0].astype(BF16), w_ref[0])


def _heads_matmul(x, w):
    h, m, k = x.shape
    n = w.shape[-1]
    return pl.pallas_call(
        _heads_matmul_body, grid=(h,),
        in_specs=[pl.BlockSpec((1, m, k), lambda i: (i, 0, 0)), pl.BlockSpec((1, k, n), lambda i: (i, 0, 0))],
        out_specs=pl.BlockSpec((1, m, n), lambda i: (i, 0, 0)),
        out_shape=jax.ShapeDtypeStruct((h, m, n), F32),
        compiler_params=_cparams("arbitrary"), name="mla_uv",
    )(x, w)


def _oproj_body(x_ref, o_ref, wo_ref, gc_ref, wcq_ref, *refs, cross):
    if cross:
        mk_ref, mv_ref, x1_ref, oc_ref = refs
    else:
        x1_ref, qc_ref = refs
    x1 = x_ref[...] + _dot(o_ref[...].astype(BF16), wo_ref[...])
    x1_ref[...] = x1
    qc = _dot(_rms(x1, gc_ref[...]).astype(BF16), wcq_ref[...]) * (MEM_HEAD_DIM ** -0.5 * LOG2E)
    if not cross:
        qc_ref[...] = qc
        return
    qc = qc.astype(BF16)
    for hh in range(MEM_HEADS):
        sl = slice(hh * MEM_HEAD_DIM, (hh + 1) * MEM_HEAD_DIM)
        s = _dot_nt(qc[:, sl], mk_ref[:, sl])
        p = jnp.exp2(s - jnp.max(s, axis=-1, keepdims=True))
        o = _dot(p.astype(BF16), mv_ref[:, sl]) / jnp.sum(p, axis=-1, keepdims=True)
        oc_ref[:, sl] = o.astype(oc_ref.dtype)


def _oproj(x, o, w_o, g_c, w_cq, *, mem=None, seq_len=1):
    m = x.shape[0]
    tm = _row_tile(min(m, seq_len) if mem is not None else m, 512)
    cross = mem is not None
    in_specs = [pl.BlockSpec((tm, D_MODEL), lambda i: (i, 0)), pl.BlockSpec((tm, o.shape[1]), lambda i: (i, 0)),
                _full(w_o.shape), _full((1, D_MODEL)), _full(w_cq.shape)]
    args = [x, o, w_o, g_c.reshape(1, D_MODEL), w_cq]
    if cross:
        n_mem = mem[0].shape[0] // (m // seq_len)
        tiles_per_seq = seq_len // tm
        for a in mem:
            in_specs.append(pl.BlockSpec((n_mem, MEM_W), lambda i: (i // tiles_per_seq, 0)))
            args.append(a)
    out_shape = [jax.ShapeDtypeStruct((m, D_MODEL), F32),
                 jax.ShapeDtypeStruct((m, MEM_W), BF16 if cross else F32)]
    out_specs = [pl.BlockSpec((tm, D_MODEL), lambda i: (i, 0)), pl.BlockSpec((tm, MEM_W), lambda i: (i, 0))]
    return pl.pallas_call(
        functools.partial(_oproj_body, cross=cross), grid=(m // tm,),
        in_specs=in_specs, out_specs=out_specs, out_shape=out_shape,
        compiler_params=_cparams("arbitrary"), name="oproj_cross" if cross else "oproj",
    )(*args)


def _ffn_body(*refs, seq_mode, final, nseq_tiles, tm, fc):
    it = iter(refs)
    x1_ref, oc_ref, wco_ref, gf_ref = next(it), next(it), next(it), next(it)
    wg_ref, wv_ref, wc_ref, bc_ref, wout_ref = next(it), next(it), next(it), next(it), next(it)
    if not seq_mode:
        b0_ref, b1_ref = next(it), next(it)
    if final:
        gfin_ref = next(it)
    x3_ref, gate_ref = next(it), next(it)
    if final:
        y_ref = next(it)
    x2_sc, h_sc, acc_sc = next(it), next(it), next(it)
    if seq_mode:
        halo_sc, gs_sc = next(it), next(it)
    i = pl.program_id(0)
    f = pl.program_id(1)

    @pl.when(f == 0)
    def _():
        x2 = x1_ref[...] + _dot(oc_ref[...].astype(BF16), wco_ref[...])
        x2_sc[...] = x2
        h_sc[...] = _rms(x2, gf_ref[...]).astype(BF16)
        acc_sc[...] = jnp.zeros_like(acc_sc)

    h = h_sc[...]
    gate = _dot(h, wg_ref[...])
    val = _dot(h, wv_ref[...])
    wc = wc_ref[...]
    if seq_mode:
        @pl.when(i % nseq_tiles == 0)
        def _():
            gs_sc[0:SUBLANES, :] = jnp.zeros((SUBLANES, fc), F32)

        @pl.when(i % nseq_tiles != 0)
        def _():
            gs_sc[0:SUBLANES, :] = halo_sc[f]

        gs_sc[SUBLANES:, :] = gate
        gm1 = gs_sc[SUBLANES - 1:SUBLANES - 1 + tm, :]
        gm2 = gs_sc[SUBLANES - 2:SUBLANES - 2 + tm, :]
        tail = gate[tm - SUBLANES:, :]
        halo_sc[f] = tail
        gate_ref[0] = tail
    else:
        gm2, gm1 = b0_ref[...], b1_ref[...]
        gate_ref[...] = gate
    conv = bc_ref[...] + wc[0:1, :] * gm2 + wc[1:2, :] * gm1 + wc[2:3, :] * gate
    y = 0.5 * conv * (1.0 + lax.erf(conv * (2.0 ** -0.5))) * val
    acc_sc[...] = acc_sc[...] + _dot(y.astype(BF16), wout_ref[...])

    @pl.when(f == pl.num_programs(1) - 1)
    def _():
        x3 = x2_sc[...] + acc_sc[...]
        x3_ref[...] = x3
        if final:
            y_ref[...] = _rms(x3, gfin_ref[...])


def _ffn(x1, oc, w_co, g_f, w_in, w_conv, b_conv, w_out, *, seq_len=1, buf=None, g_final=None):
    m = x1.shape[0]
    seq_mode = buf is None
    final = g_final is not None
    tm = _row_tile(min(m, seq_len) if seq_mode else m, 512)
    nseq_tiles = max(seq_len // tm, 1)
    fc = D_FF
    nf = D_FF // fc
    row = lambda i, f: (i, 0)
    in_specs = [pl.BlockSpec((tm, D_MODEL), row), pl.BlockSpec((tm, MEM_W), row), _full(w_co.shape),
                _full((1, D_MODEL)),
                pl.BlockSpec((D_MODEL, fc), lambda i, f: (0, f)),
                pl.BlockSpec((D_MODEL, fc), lambda i, f: (0, nf + f)),
                pl.BlockSpec((CONV_W, fc), lambda i, f: (0, f)),
                pl.BlockSpec((1, fc), lambda i, f: (0, f)),
                pl.BlockSpec((fc, D_MODEL), lambda i, f: (f, 0))]
    args = [x1, oc, w_co, g_f.reshape(1, D_MODEL), w_in, w_in, w_conv, b_conv.reshape(1, D_FF), w_out]
    if not seq_mode:
        in_specs += [pl.BlockSpec((tm, fc), lambda i, f: (i, f))] * 2
        args += [buf[0], buf[1]]
    if final:
        in_specs.append(_full((1, D_MODEL)))
        args.append(g_final.reshape(1, D_MODEL))
    out_shape = [jax.ShapeDtypeStruct((m, D_MODEL), F32)]
    out_specs = [pl.BlockSpec((tm, D_MODEL), row)]
    if seq_mode:
        out_shape.append(jax.ShapeDtypeStruct((m // tm, SUBLANES, D_FF), F32))
        out_specs.append(pl.BlockSpec((1, SUBLANES, fc), lambda i, f: (i, 0, f)))
    else:
        out_shape.append(jax.ShapeDtypeStruct((m, D_FF), F32))
        out_specs.append(pl.BlockSpec((tm, fc), lambda i, f: (i, f)))
    if final:
        out_shape.append(jax.ShapeDtypeStruct((m, D_MODEL), F32))
        out_specs.append(pl.BlockSpec((tm, D_MODEL), row))
    scratch = [pltpu.VMEM((tm, D_MODEL), F32), pltpu.VMEM((tm, D_MODEL), BF16), pltpu.VMEM((tm, D_MODEL), F32)]
    if seq_mode:
        scratch += [pltpu.VMEM((nf, SUBLANES, fc), F32), pltpu.VMEM((tm + SUBLANES, fc), F32)]
    return pl.pallas_call(
        functools.partial(_ffn_body, seq_mode=seq_mode, final=final, nseq_tiles=nseq_tiles, tm=tm, fc=fc),
        grid=(m // tm, nf), in_specs=in_specs, out_specs=out_specs, out_shape=out_shape,
        scratch_shapes=scratch, compiler_params=_cparams("arbitrary", "arbitrary"),
        name="ffn_seq" if seq_mode else "ffn_tok",
    )(*args)


def _rope_tables(pos, half, period, lane_off):
    inv = jnp.exp(-math.log(ROPE_THETA) * jnp.arange(half, dtype=F32) / half)
    ang = pos.astype(F32)[:, None] * inv[None, :]
    cos, sin = jnp.cos(ang), jnp.sin(ang)
    n = pos.shape[0]
    one = jnp.ones((n, lane_off), F32)
    zero = jnp.zeros((n, lane_off), F32)
    padw = period - lane_off - 2 * half
    pad1, pad0 = jnp.ones((n, padw), F32), jnp.zeros((n, padw), F32)
    z = jnp.zeros_like(sin)
    cos_t = jnp.concatenate([one, cos, cos, pad1], axis=1)
    sin_a = jnp.concatenate([zero, -sin, z, pad0], axis=1)
    sin_b = jnp.concatenate([zero, z, sin, pad0], axis=1)
    rep = LANES // period
    return tuple(jnp.tile(t, (1, rep)) for t in (cos_t, sin_a, sin_b))


def _mla_weights(w_in, g_q, g_kv, w_uq, w_uk, w_uv):
    kr_cols = w_in[:, MLA_Q_RANK + MLA_KV_RANK:]
    slab = jnp.concatenate([jnp.zeros((D_MODEL, MLA_NOPE), F32), kr_cols,
                            jnp.zeros((D_MODEL, LANES - MLA_QK), F32)], axis=1)
    w_in_p = jnp.concatenate([w_in[:, :MLA_Q_RANK + MLA_KV_RANK], slab], axis=1)
    uq = w_uq.reshape(MLA_Q_RANK, N_HEADS, MLA_QK)
    uq = jnp.pad(uq, ((0, 0), (0, 0), (0, LANES - MLA_QK))).reshape(MLA_Q_RANK, N_HEADS * LANES)
    uk = jnp.pad(w_uk, ((0, 0), (0, 0), (0, LANES - MLA_NOPE))).reshape(MLA_KV_RANK, N_HEADS * LANES)
    return dict(
        w_in=w_in_p.astype(BF16), g_q=g_q.reshape(1, MLA_Q_RANK), g_kv=g_kv.reshape(1, MLA_KV_RANK),
        w_uq=uq.astype(BF16), w_uk=uk.astype(BF16),
        w_uv=w_uv.reshape(MLA_KV_RANK, N_HEADS * MLA_V).astype(BF16),
        w_ukt=jnp.transpose(w_uk, (1, 2, 0)).astype(BF16),
        w_uvh=jnp.transpose(w_uv, (1, 0, 2)).astype(BF16),
    )


def _repeat_q(q_hm, n_rep):
    q = jnp.transpose(q_hm, (1, 0, 2)).astype(F32)
    return jnp.tile(q, (1, 1, n_rep))


def _kv_pages_t(cache):
    p, s = cache.shape[0], cache.shape[1]
    return jnp.transpose(cache, (0, 2, 3, 1)).reshape(p, KV_W, s)


def kernel(x_prompt, x_sample, cache_sb_k, cache_sb_v, cache_mla_ckv, cache_mla_kr, cache_fox_k, cache_fox_v, cache_fox_logf, state_swa_k, state_swa_v, cache_mem_k, cache_mem_v, state_ffn_conv, page_table, mem_prompt, g_mix, w_sb_qkv, w_sb_o, w_mla_in, g_mla_q, g_mla_kv, w_mla_uq, w_mla_uk, w_mla_uv, w_mla_o, w_fox_in, b_fox_f, w_fox_o, w_swa_qkv, swa_sinks, w_swa_o, g_cross, g_mem, w_cq, w_ckv, w_co, g_ffn, w_ffn_in, w_ffn_conv, b_ffn_conv, w_ffn_out, g_final):
    bp, seq, _ = x_prompt.shape
    bs = x_sample.shape[0]
    n_pages = page_table.shape[1]
    past = n_pages * PAGE_SIZE
    depth = g_mix.shape[0]
    n_mem = mem_prompt.shape[1]
    mp = bp * seq
    win = state_swa_k.shape[2]
    pb = math.gcd(n_pages, 32)

    bf = lambda a: a.astype(BF16)
    pos_p = jnp.arange(seq)
    pos_s = jnp.full((bs,), past, jnp.int32)
    swa_tabs_p = _rope_tables(pos_p, HEAD_DIM // 2, HEAD_DIM, 0)
    swa_tabs_s = _rope_tables(pos_s, HEAD_DIM // 2, HEAD_DIM, 0)
    mla_tabs_p = _rope_tables(pos_p, MLA_ROPE // 2, LANES, MLA_NOPE)
    mla_tabs_s = _rope_tables(pos_s, MLA_ROPE // 2, LANES, MLA_NOPE)
    ident_pages = jnp.arange(bs, dtype=jnp.int32).reshape(bs, 1)
    mem_k_rows = cache_mem_k.reshape(depth * bs, n_mem * MEM_HEADS, MEM_HEAD_DIM)
    mem_v_rows = cache_mem_v.reshape(depth * bs, n_mem * MEM_HEADS, MEM_HEAD_DIM)

    outs = {}
    xp = x_prompt.reshape(mp, D_MODEL)
    xs = x_sample.reshape(bs, D_MODEL)
    mem_flat = mem_prompt.reshape(bp * n_mem, D_MODEL)
    mem_k_p, mem_v_p, conv_p, conv_s = [], [], [], []
    y_p = y_s = None

    for i in range(depth):
        m, j = i % 4, i // 4
        gm = g_mix[i]
        if m == 0:
            w = bf(w_sb_qkv[j])
            q, k, v, kh, vh = _proj_qkv(xp, gm, w)
            o_p = _attn_sb(q, kh, vh, bp, seq)
            outs.setdefault("sb_k_p", []).append(k.reshape(bp, seq, N_KV, HEAD_DIM))
            outs.setdefault("sb_v_p", []).append(v.reshape(bp, seq, N_KV, HEAD_DIM))
            q, k, v, kh, vh = _proj_qkv(xs, gm, w)
            o_s = _decode("sb", _repeat_q(q, N_KV), _kv_pages_t(cache_sb_k[j]), _kv_pages_t(cache_sb_v[j]),
                          page_table, pb=pb)
            o_s = o_s.reshape(bs, Q_W)
            outs.setdefault("sb_k_s", []).append(k.reshape(bs, 1, N_KV, HEAD_DIM))
            outs.setdefault("sb_v_s", []).append(v.reshape(bs, 1, N_KV, HEAD_DIM))
            w_o = bf(w_sb_o[j])
        elif m == 1:
            wp = _mla_weights(w_mla_in[j], g_mla_q[j], g_mla_kv[j], w_mla_uq[j], w_mla_uk[j], w_mla_uv[j])
            q, kh, vh, ckv, kr = _proj_mla(xp, gm, wp, mla_tabs_p, absorbed=False)
            o_p = _attn("mla", q, kh, vh, bp, seq)
            outs.setdefault("mla_ckv_p", []).append(ckv.reshape(bp, seq, MLA_KV_RANK))
            outs.setdefault("mla_kr_p", []).append(kr[:, MLA_NOPE:MLA_QK].reshape(bp, seq, MLA_ROPE))
            q, kh, vh, ckv, kr, qlat = _proj_mla(xs, gm, wp, mla_tabs_s, absorbed=True)
            kr32 = kr[:, MLA_NOPE:MLA_QK]
            qr = jnp.transpose(q[:, :, MLA_NOPE:MLA_QK], (1, 0, 2)).astype(F32)
            o_lat = _decode("mla", jnp.transpose(qlat, (1, 0, 2)), cache_mla_ckv[j], None, page_table,
                            pb=pb, qr=qr, kr_pool=jnp.transpose(cache_mla_kr[j], (0, 2, 1)),
                            k_new=ckv.reshape(bs, 1, MLA_KV_RANK), kr_new=kr32.reshape(bs, 1, MLA_ROPE))
            o_s = _heads_matmul(jnp.transpose(o_lat, (1, 0, 2)), wp["w_uvh"])
            o_s = jnp.transpose(o_s, (1, 0, 2)).reshape(bs, Q_W)
            outs.setdefault("mla_ckv_s", []).append(ckv.reshape(bs, 1, MLA_KV_RANK))
            outs.setdefault("mla_kr_s", []).append(kr32.reshape(bs, 1, MLA_ROPE))
            w_o = bf(w_mla_o[j])
        elif m == 2:
            w = bf(w_fox_in[j])
            q, k, v, kh, vh, lf, c = _proj_qkv(xp, gm, w, b_f=b_fox_f[j], seq_len=seq)
            cq = jnp.transpose(c.reshape(mp, N_KV, GROUP), (1, 0, 2))
            ck = jnp.transpose(c.reshape(mp, N_KV, GROUP), (1, 2, 0))
            o_p = _attn("fox", q, kh, vh, bp, seq, cq=cq, ck=ck)
            outs.setdefault("fox_k_p", []).append(k.reshape(bp, seq, N_KV, HEAD_DIM))
            outs.setdefault("fox_v_p", []).append(v.reshape(bp, seq, N_KV, HEAD_DIM))
            outs.setdefault("fox_lf_p", []).append(lf.reshape(bp, seq, N_HEADS))
            q, k, v, kh, vh, lf, c = _proj_qkv(xs, gm, w, b_f=b_fox_f[j], seq_len=1)
            o_s = _decode("fox", _repeat_q(q, N_KV), _kv_pages_t(cache_fox_k[j]), _kv_pages_t(cache_fox_v[j]),
                          page_table, pb=pb, lf_pool=jnp.transpose(cache_fox_logf[j], (0, 2, 1)),
                          k_new=k.reshape(bs, 1, KV_W), v_new=v.reshape(bs, 1, KV_W),
                          lf_new=lf.reshape(bs, N_HEADS, 1))
            o_s = o_s.reshape(bs, Q_W)
            outs.setdefault("fox_k_s", []).append(k.reshape(bs, 1, N_KV, HEAD_DIM))
            outs.setdefault("fox_v_s", []).append(v.reshape(bs, 1, N_KV, HEAD_DIM))
            outs.setdefault("fox_lf_s", []).append(lf.reshape(bs, 1, N_HEADS))
            w_o = bf(w_fox_o[j])
        else:
            w = bf(w_swa_qkv[j])
            q, k, v, kh, vh = _proj_qkv(xp, gm, w, rope_tabs=swa_tabs_p)
            o_p = _attn("swa", q, kh, vh, bp, seq, sinks=swa_sinks[j])
            outs.setdefault("swa_k_p", []).append(k.reshape(bp, seq, N_KV, HEAD_DIM)[:, seq - win:])
            outs.setdefault("swa_v_p", []).append(v.reshape(bp, seq, N_KV, HEAD_DIM)[:, seq - win:])
            q, k, v, kh, vh = _proj_qkv(xs, gm, w, rope_tabs=swa_tabs_s)
            o_s = _decode("swa", _repeat_q(q, N_KV), _kv_pages_t(state_swa_k[j]), _kv_pages_t(state_swa_v[j]),
                          ident_pages, pb=1, k_new=k.reshape(bs, 1, KV_W), v_new=v.reshape(bs, 1, KV_W),
                          sinks=swa_sinks[j])
            o_s = o_s.reshape(bs, Q_W)
            k_all = jnp.concatenate([state_swa_k[j], k.reshape(bs, 1, N_KV, HEAD_DIM)], axis=1)
            v_all = jnp.concatenate([state_swa_v[j], v.reshape(bs, 1, N_KV, HEAD_DIM)], axis=1)
            outs.setdefault("swa_k_s", []).append(k_all[:, -win:])
            outs.setdefault("swa_v_s", []).append(v_all[:, -win:])
            w_o = bf(w_swa_o[j])

        kv_f32, kv_b16 = _proj_plain(mem_flat, g_mem[i], bf(w_ckv[i]))
        mem_k_p.append(kv_f32[:, :MEM_W].reshape(bp, n_mem, MEM_HEADS, MEM_HEAD_DIM))
        mem_v_p.append(kv_f32[:, MEM_W:].reshape(bp, n_mem, MEM_HEADS, MEM_HEAD_DIM))
        w_cq_b, w_co_b = bf(w_cq[i]), bf(w_co[i])
        x1_p, oc_p = _oproj(xp, o_p, w_o, g_cross[i], w_cq_b,
                            mem=(kv_b16[:, :MEM_W], kv_b16[:, MEM_W:]), seq_len=seq)
        x1_s, qc_s = _oproj(xs, o_s, w_o, g_cross[i], w_cq_b)
        qc_rows = jnp.pad(qc_s.reshape(bs, MEM_HEADS, MEM_HEAD_DIM), ((0, 0), (0, SUBLANES - MEM_HEADS), (0, 0)))
        oc_s = _cross_decode(qc_rows, mem_k_rows, mem_v_rows, i)
        oc_s = oc_s[:, :MEM_HEADS].reshape(bs, MEM_W)

        fin = g_final if i == depth - 1 else None
        w_in_b, w_out_b = bf(w_ffn_in[i]), bf(w_ffn_out[i])
        res = _ffn(x1_p, oc_p, w_co_b, g_ffn[i], w_in_b, w_ffn_conv[i], b_ffn_conv[i], w_out_b,
                   seq_len=seq, g_final=fin)
        xp, tails = res[0], res[1]
        if fin is not None:
            y_p = res[2]
        tps = tails.shape[0] // bp
        conv_p.append(tails.reshape(bp, tps, SUBLANES, D_FF)[:, -1, SUBLANES - (CONV_W - 1):])
        buf = state_ffn_conv[i]
        res = _ffn(x1_s, oc_s, w_co_b, g_ffn[i], w_in_b, w_ffn_conv[i], b_ffn_conv[i], w_out_b,
                   buf=(buf[:, 0], buf[:, 1]), g_final=fin)
        xs, gate_s = res[0], res[1]
        if fin is not None:
            y_s = res[2]
        conv_s.append(jnp.stack([buf[:, 1], gate_s], axis=1))

    stk = lambda name: jnp.stack(outs[name])
    return (y_p.reshape(bp, seq, D_MODEL), y_s.reshape(bs, 1, D_MODEL),
            stk("sb_k_p"), stk("sb_v_p"), stk("sb_k_s"), stk("sb_v_s"),
            stk("mla_ckv_p"), stk("mla_kr_p"), stk("mla_ckv_s"), stk("mla_kr_s"),
            stk("fox_k_p"), stk("fox_v_p"), stk("fox_lf_p"),
            stk("fox_k_s"), stk("fox_v_s"), stk("fox_lf_s"),
            stk("swa_k_p"), stk("swa_v_p"), stk("swa_k_s"), stk("swa_v_s"),
            jnp.stack(mem_k_p), jnp.stack(mem_v_p),
            jnp.stack(conv_p), jnp.stack(conv_s))
```

```python
import functools
import math

import jax
import jax.numpy as jnp
from jax import lax
from jax.experimental import pallas as pl
from jax.experimental.pallas import tpu as pltpu

F32 = jnp.float32
BF16 = jnp.bfloat16

D_MODEL = 1024
N_HEADS = 16
HEAD_DIM = 64
N_KV = 4
GROUP = N_HEADS // N_KV
Q_W = N_HEADS * HEAD_DIM
KV_W = N_KV * HEAD_DIM
QKV_W = Q_W + 2 * KV_W
MLA_Q_RANK = 768
MLA_KV_RANK = 256
MLA_NOPE = 64
MLA_ROPE = 32
MLA_V = 64
MLA_QK = MLA_NOPE + MLA_ROPE
WINDOW = 128
MEM_HEADS = 4
MEM_HEAD_DIM = 128
MEM_W = MEM_HEADS * MEM_HEAD_DIM
D_FF = 2816
CONV_W = 3
ROPE_THETA = 10000.0
EPS = 1e-6
PAGE_SIZE = 128

LANES = 128
SUBLANES = 8
NEG_BIG = -1e30
VMEM_LIMIT = 56 * 1024 * 1024
CUM_CHUNK = 256
LOG2E = math.log2(math.e)
V_ONE = HEAD_DIM


def _cparams(*sem):
    return pltpu.CompilerParams(dimension_semantics=sem, vmem_limit_bytes=VMEM_LIMIT)


def _full(shape):
    return pl.BlockSpec(shape, lambda *_: (0,) * len(shape))


def _row_tile(m, pref):
    t = min(m, pref)
    assert m % t == 0, (m, t)
    return t


def _rms(x, g):
    return x * lax.rsqrt(jnp.mean(x * x, axis=-1, keepdims=True) + EPS) * g


def _dot(a, b):
    return jnp.dot(a, b, preferred_element_type=F32)


def _dot_nt(a, b):
    return lax.dot_general(a, b, (((1,), (1,)), ((), ())), preferred_element_type=F32)


def _log_sigmoid(z):
    return jnp.minimum(z, 0.0) - jnp.log1p(jnp.exp(-jnp.abs(z)))


def _log2_one_minus_sigmoid(z2):
    nz = -z2
    return jnp.minimum(nz, 0.0) - jnp.log2(1.0 + jnp.exp2(jnp.minimum(z2, nz)))


def _rope_slab(x, cos_t, sin_a, sin_b, half):
    return (x * cos_t + pltpu.roll(x, LANES - half, axis=1) * sin_a
            + pltpu.roll(x, half, axis=1) * sin_b)


def _rope_wide(x, cos_t, sin_a, sin_b, half):
    n = x.shape[1] // LANES
    return jnp.concatenate(
        [_rope_slab(x[:, i * LANES:(i + 1) * LANES], cos_t, sin_a, sin_b, half) for i in range(n)],
        axis=1)


def _split2(x):
    a = x.astype(BF16)
    b = (x - a.astype(F32)).astype(BF16)
    return a, b


def _split3(x):
    a = x.astype(BF16)
    r = x - a.astype(F32)
    b = r.astype(BF16)
    c = (r - b.astype(F32)).astype(BF16)
    return a, b, c


def _store_heads(dst_ref, a, n_heads, scale=None):
    for h in range(n_heads):
        blk = a[:, h * HEAD_DIM:(h + 1) * HEAD_DIM]
        if scale is not None:
            blk = blk * scale
        dst_ref[h] = blk.astype(dst_ref.dtype)


def _store_v_heads(dst_ref, v, n_heads):
    rows = v.shape[0]
    one_col = (lax.broadcasted_iota(jnp.int32, (rows, LANES - HEAD_DIM), 1) == 0).astype(F32)
    for h in range(n_heads):
        blk = jnp.concatenate([v[:, h * HEAD_DIM:(h + 1) * HEAD_DIM], one_col], axis=1)
        dst_ref[h] = blk.astype(dst_ref.dtype)


def _proj_qkv_body(*refs, rope, fox, nseq_tiles):
    it = iter(refs)
    x_ref, g_ref, w_ref = next(it), next(it), next(it)
    if rope:
        cos_ref, sa_ref, sb_ref = next(it), next(it), next(it)
    if fox:
        bf_ref, tri_ref = next(it), next(it)
    q_ref, k_ref, v_ref, kh_ref, vh_ref = next(it), next(it), next(it), next(it), next(it)
    if fox:
        lf_ref, c_ref, carry_sc = next(it), next(it), next(it)

    h = _rms(x_ref[...], g_ref[...]).astype(BF16)
    a = _dot(h, w_ref[...])
    q = a[:, :Q_W]
    k = a[:, Q_W:Q_W + KV_W]
    v = a[:, Q_W + KV_W:QKV_W]
    if rope:
        cos_t, sin_a, sin_b = cos_ref[...], sa_ref[...], sb_ref[...]
        q = _rope_wide(q, cos_t, sin_a, sin_b, HEAD_DIM // 2)
        k = _rope_wide(k, cos_t, sin_a, sin_b, HEAD_DIM // 2)
    k_ref[...] = k
    v_ref[...] = v
    _store_heads(q_ref, q, N_HEADS, scale=HEAD_DIM ** -0.5 * LOG2E)
    _store_heads(kh_ref, k, N_KV)
    _store_v_heads(vh_ref, v, N_KV)
    if fox:
        @pl.when(pl.program_id(0) % nseq_tiles == 0)
        def _():
            carry_sc[...] = jnp.zeros_like(carry_sc)

        lf = _log_sigmoid(a[:, QKV_W:QKV_W + N_HEADS] + bf_ref[...])
        lf_ref[...] = lf
        tri = tri_ref[...]
        t3 = _split3(lf)
        cs = _dot(tri, t3[0]) + _dot(tri, t3[1]) + _dot(tri, t3[2]) + carry_sc[...]
        c_ref[...] = cs * LOG2E
        carry_sc[...] = cs[cs.shape[0] - 1:, :]


def _proj_qkv(x, g, w, *, rope_tabs=None, b_f=None, seq_len=1):
    m = x.shape[0]
    rope = rope_tabs is not None
    fox = b_f is not None
    tm = _row_tile(min(m, seq_len) if fox and seq_len > 1 else m, 512)
    nseq_tiles = max(seq_len // tm, 1)
    in_specs = [pl.BlockSpec((tm, D_MODEL), lambda i: (i, 0)), _full((1, D_MODEL)), _full(w.shape)]
    args = [x, g.reshape(1, D_MODEL), w]
    if rope:
        nt = rope_tabs[0].shape[0] // tm
        for t in rope_tabs:
            in_specs.append(pl.BlockSpec((tm, LANES), lambda i, nt=nt: (i % nt, 0)))
            args.append(t)
    if fox:
        in_specs += [_full((1, N_HEADS)), _full((tm, tm))]
        args += [b_f.reshape(1, N_HEADS), jnp.tril(jnp.ones((tm, tm), F32)).astype(BF16)]
    out_shape = [
        jax.ShapeDtypeStruct((N_HEADS, m, HEAD_DIM), BF16),
        jax.ShapeDtypeStruct((m, KV_W), F32),
        jax.ShapeDtypeStruct((m, KV_W), F32),
        jax.ShapeDtypeStruct((N_KV, m, HEAD_DIM), BF16),
        jax.ShapeDtypeStruct((N_KV, m, LANES), BF16),
    ]
    out_specs = [
        pl.BlockSpec((N_HEADS, tm, HEAD_DIM), lambda i: (0, i, 0)),
        pl.BlockSpec((tm, KV_W), lambda i: (i, 0)),
        pl.BlockSpec((tm, KV_W), lambda i: (i, 0)),
        pl.BlockSpec((N_KV, tm, HEAD_DIM), lambda i: (0, i, 0)),
        pl.BlockSpec((N_KV, tm, LANES), lambda i: (0, i, 0)),
    ]
    scratch = []
    if fox:
        out_shape += [jax.ShapeDtypeStruct((m, N_HEADS), F32)] * 2
        out_specs += [pl.BlockSpec((tm, N_HEADS), lambda i: (i, 0))] * 2
        scratch = [pltpu.VMEM((1, N_HEADS), F32)]
    return pl.pallas_call(
        functools.partial(_proj_qkv_body, rope=rope, fox=fox, nseq_tiles=nseq_tiles),
        grid=(m // tm,), in_specs=in_specs, out_specs=out_specs, out_shape=out_shape,
        scratch_shapes=scratch, compiler_params=_cparams("arbitrary"),
        name="proj_fox" if fox else ("proj_swa" if rope else "proj_sb"),
    )(*args)


def _proj_mla_body(x_ref, g_ref, win_ref, gq_ref, gkv_ref, wuq_ref, wuk_ref, wuv_ref, wukt_ref,
                   cos_ref, sa_ref, sb_ref,
                   q_ref, kh_ref, vh_ref, ckv_ref, kr_ref, *maybe_qlat, absorbed):
    scale = MLA_QK ** -0.5 * LOG2E
    half = MLA_ROPE // 2
    cos_t, sin_a, sin_b = cos_ref[...], sa_ref[...], sb_ref[...]
    h = _rms(x_ref[...], g_ref[...]).astype(BF16)
    a = _dot(h, win_ref[...])
    cq = _rms(a[:, :MLA_Q_RANK], gq_ref[...]).astype(BF16)
    ckv = _rms(a[:, MLA_Q_RANK:MLA_Q_RANK + MLA_KV_RANK], gkv_ref[...])
    ckv_ref[...] = ckv
    ckv_b = ckv.astype(BF16)
    kr = _rope_slab(a[:, MLA_Q_RANK + MLA_KV_RANK:], cos_t, sin_a, sin_b, half)
    kr_ref[...] = kr
    qf = _dot(cq, wuq_ref[...])
    kn = _dot(ckv_b, wuk_ref[...])
    _store_v_heads(vh_ref, _dot(ckv_b, wuv_ref[...]), N_HEADS)
    for hd in range(N_HEADS):
        sl = slice(hd * LANES, (hd + 1) * LANES)
        qh = _rope_slab(qf[:, sl], cos_t, sin_a, sin_b, half) * scale
        q_ref[hd] = qh.astype(BF16)
        kh_ref[hd] = (kn[:, sl] + kr).astype(BF16)
        if absorbed:
            qn = qh[:, :MLA_NOPE].astype(BF16)
            maybe_qlat[0][hd] = _dot(qn, wukt_ref[hd])


def _proj_mla(x, g, wp, rope_tabs, *, absorbed):
    m = x.shape[0]
    tm = _row_tile(m, 512)
    nt = rope_tabs[0].shape[0] // tm
    args = [x, g.reshape(1, D_MODEL), wp["w_in"], wp["g_q"], wp["g_kv"], wp["w_uq"], wp["w_uk"],
            wp["w_uv"], wp["w_ukt"]]
    in_specs = [pl.BlockSpec((tm, D_MODEL), lambda i: (i, 0))] + [_full(a.shape) for a in args[1:]]
    for t in rope_tabs:
        in_specs.append(pl.BlockSpec((tm, LANES), lambda i, nt=nt: (i % nt, 0)))
        args.append(t)
    out_shape = [
        jax.ShapeDtypeStruct((N_HEADS, m, LANES), BF16),
        jax.ShapeDtypeStruct((N_HEADS, m, LANES), BF16),
        jax.ShapeDtypeStruct((N_HEADS, m, LANES), BF16),
        jax.ShapeDtypeStruct((m, MLA_KV_RANK), F32),
        jax.ShapeDtypeStruct((m, LANES), F32),
    ]
    hm = pl.BlockSpec((N_HEADS, tm, LANES), lambda i: (0, i, 0))
    out_specs = [hm, hm, hm,
                 pl.BlockSpec((tm, MLA_KV_RANK), lambda i: (i, 0)),
                 pl.BlockSpec((tm, LANES), lambda i: (i, 0))]
    if absorbed:
        out_shape.append(jax.ShapeDtypeStruct((N_HEADS, m, MLA_KV_RANK), F32))
        out_specs.append(pl.BlockSpec((N_HEADS, tm, MLA_KV_RANK), lambda i: (0, i, 0)))
    return pl.pallas_call(
        functools.partial(_proj_mla_body, absorbed=absorbed),
        grid=(m // tm,), in_specs=in_specs, out_specs=out_specs, out_shape=out_shape,
        compiler_params=_cparams("arbitrary"), name="proj_mla",
    )(*args)


def _proj_plain_body(x_ref, g_ref, w_ref, o_ref, ob_ref):
    a = _dot(_rms(x_ref[...], g_ref[...]).astype(BF16), w_ref[...])
    o_ref[...] = a
    ob_ref[...] = a.astype(BF16)


def _proj_plain(x, g, w):
    m, n = x.shape[0], w.shape[1]
    tm = _row_tile(m, 512)
    return pl.pallas_call(
        _proj_plain_body, grid=(m // tm,),
        in_specs=[pl.BlockSpec((tm, D_MODEL), lambda i: (i, 0)), _full((1, D_MODEL)), _full(w.shape)],
        out_specs=[pl.BlockSpec((tm, n), lambda i: (i, 0))] * 2,
        out_shape=[jax.ShapeDtypeStruct((m, n), F32), jax.ShapeDtypeStruct((m, n), BF16)],
        compiler_params=_cparams("arbitrary"), name="proj_mem",
    )(x, g.reshape(1, D_MODEL), w)


FLAG_FIRST, FLAG_LAST, FLAG_MASK = 1, 2, 4


def _schedule(n_tiles, tile, *, window=None, descending=False, strict=False):
    qi, kj, fl = [], [], []
    for i in range(n_tiles):
        q_lo = i * tile
        lo = 0 if window is None else max(0, (q_lo - window) // tile)
        js = list(range(lo, i + 1))
        if descending:
            js = js[::-1]
        for n, j in enumerate(js):
            k_hi = (j + 1) * tile - 1
            need_mask = (k_hi >= q_lo) if strict else (k_hi > q_lo)
            f = (FLAG_FIRST if n == 0 else 0) | (FLAG_LAST if n == len(js) - 1 else 0)
            f |= FLAG_MASK if need_mask else 0
            qi.append(i), kj.append(j), fl.append(f)
    return (jnp.asarray(qi, jnp.int32), jnp.asarray(kj, jnp.int32), jnp.asarray(fl, jnp.int32))


def _attn_body(qi_ref, kj_ref, fl_ref, *refs, mode, nh, tq, tk):
    it = iter(refs)
    if mode == "swa":
        sink_ref = next(it)
    q_ref, k_ref, v_ref = next(it), next(it), next(it)
    if mode == "fox":
        cq_ref, ck_ref = next(it), next(it)
    o_ref, m_sc, acc_sc = next(it), next(it), next(it)
    if mode == "fox":
        cqr_sc = next(it)
    g = pl.program_id(1)
    s_id = pl.program_id(2)
    flag = fl_ref[s_id]
    q_lo = qi_ref[s_id] * tq
    k_lo = kj_ref[s_id] * tk
    own_kv = mode == "mla"

    @pl.when((flag & FLAG_FIRST) != 0)
    def _():
        if mode == "swa":
            one_col = (lax.broadcasted_iota(jnp.int32, (tq, LANES), 1) == V_ONE).astype(F32)
            for hh in range(nh):
                m_sc[hh] = jnp.full((tq, LANES), sink_ref[g * nh + hh] * LOG2E, F32)
                acc_sc[hh] = one_col
        else:
            m_sc[...] = jnp.full(m_sc.shape, NEG_BIG, F32)
            acc_sc[...] = jnp.zeros_like(acc_sc)
        if mode == "fox":
            for hh in range(nh):
                cqr_sc[hh] = jnp.broadcast_to(cq_ref[0][:, hh:hh + 1], (tq, LANES))

    def run_blocks(blocks):
        for row0, nrows, col0, ncols, masked in blocks:
            rs, cs = slice(row0, row0 + nrows), slice(col0, col0 + ncols)
            if masked:
                qpos = q_lo + row0 + lax.broadcasted_iota(jnp.int32, (nrows, ncols), 0)
                kpos = k_lo + col0 + lax.broadcasted_iota(jnp.int32, (nrows, ncols), 1)
                vis = kpos <= qpos
                if mode == "swa":
                    vis = vis & (qpos - kpos <= WINDOW)
            for hh in range(nh):
                kv = hh if own_kv else 0
                s = _dot_nt(q_ref[hh, rs, :], k_ref[kv, cs, :])
                if mode == "fox":
                    s = s - ck_ref[0, hh:hh + 1, cs]
                if masked:
                    s = jnp.where(vis, s, NEG_BIG)
                m_prev = m_sc[hh, rs, :]
                smax = jnp.max(s, axis=-1, keepdims=True)
                if mode == "fox":
                    cq = cqr_sc[hh, rs, :]
                    m_new = jnp.maximum(m_prev, smax + cq)
                    shift = m_new - cq
                else:
                    m_new = jnp.maximum(m_prev, smax)
                    shift = m_new
                alpha = jnp.exp2(m_prev - m_new)
                p = jnp.exp2(s - jnp.concatenate([shift] * (ncols // LANES), axis=1))
                acc_sc[hh, rs, :] = alpha * acc_sc[hh, rs, :] + _dot(p.astype(BF16), v_ref[kv, cs, :])
                m_sc[hh, rs, :] = m_new

    if mode == "swa":
        w = WINDOW
        on_diag = [(c * w, min(2 * w, tq - c * w), c * w, w, True) for c in range(tk // w)]
        off_diag = [(0, w, tk - w, w, True)]
    else:
        h = tk // 4
        on_diag = [(c * h, tq - c * h, c * h, h, True) for c in range(4)]
        off_diag = [(0, tq, 0, tk, False)]

    @pl.when((flag & FLAG_MASK) != 0)
    def _():
        run_blocks(on_diag)

    @pl.when((flag & FLAG_MASK) == 0)
    def _():
        run_blocks(off_diag)

    @pl.when((flag & FLAG_LAST) != 0)
    def _():
        for hh in range(nh):
            acc = acc_sc[hh]
            o = acc[:, :HEAD_DIM] / acc[:, V_ONE:V_ONE + 1]
            o_ref[:, hh * HEAD_DIM:(hh + 1) * HEAD_DIM] = o.astype(o_ref.dtype)


def _attn(mode, q_hm, k_hm, v_hm, batch, seq, *, cq=None, ck=None, sinks=None):
    nh = GROUP
    tq = tk = _row_tile(seq, 512 if mode == "swa" else 1024)
    nq = seq // tq
    dk = q_hm.shape[-1]
    nkv = nh if mode == "mla" else 1
    qi, kj, fl = _schedule(nq, tq, window=WINDOW if mode == "swa" else None)
    prefetch = [qi, kj, fl]
    if mode == "swa":
        prefetch.append(sinks.astype(F32))

    def qmap(b, g, s, qi, kj, *_):
        return (g, b * nq + qi[s], 0)

    def kmap(b, g, s, qi, kj, *_):
        return (g, b * nq + kj[s], 0)

    in_specs = [pl.BlockSpec((nh, tq, dk), qmap), pl.BlockSpec((nkv, tk, dk), kmap),
                pl.BlockSpec((nkv, tk, LANES), kmap)]
    args = [q_hm, k_hm, v_hm]
    if mode == "fox":
        in_specs.append(pl.BlockSpec((1, tq, GROUP), qmap))
        in_specs.append(pl.BlockSpec((1, GROUP, tk), lambda b, g, s, qi, kj, *_: (g, 0, b * nq + kj[s])))
        args += [cq, ck]
    grid_spec = pltpu.PrefetchScalarGridSpec(
        num_scalar_prefetch=len(prefetch), grid=(batch, N_HEADS // nh, int(qi.shape[0])),
        in_specs=in_specs,
        out_specs=pl.BlockSpec((tq, nh * HEAD_DIM), lambda b, g, s, qi, kj, *_: (b * nq + qi[s], g)),
        scratch_shapes=[pltpu.VMEM((nh, tq, LANES), F32)] * (3 if mode == "fox" else 2))
    return pl.pallas_call(
        functools.partial(_attn_body, mode=mode, nh=nh, tq=tq, tk=tk),
        grid_spec=grid_spec, out_shape=jax.ShapeDtypeStruct((batch * seq, Q_W), BF16),
        compiler_params=_cparams("arbitrary", "arbitrary", "arbitrary"), name="attn_" + mode,
    )(*prefetch, *args)


def _attn_sb_body(qi_ref, kj_ref, fl_ref, q_ref, k_ref, v_ref, u_ref, o_ref, run_sc, acc_sc, *, nh, tq, tk):
    s_id = pl.program_id(2)
    flag = fl_ref[s_id]
    q_lo = qi_ref[s_id] * tq
    k_lo = kj_ref[s_id] * tk
    ck = min(CUM_CHUNK, tk)
    n_chunks = tk // ck

    @pl.when((flag & FLAG_FIRST) != 0)
    def _():
        run_sc[...] = jnp.zeros_like(run_sc)
        acc_sc[...] = jnp.zeros_like(acc_sc)

    def step(masked):
        u = u_ref[...]
        for r in reversed(range(n_chunks)):
            kc = k_ref[0, r * ck:(r + 1) * ck, :]
            vc = v_ref[0, r * ck:(r + 1) * ck, :]
            row0 = r * ck if masked else 0
            rows = tq - row0
            if masked:
                qpos = q_lo + row0 + lax.broadcasted_iota(jnp.int32, (rows, ck), 0)
                kpos = k_lo + r * ck + lax.broadcasted_iota(jnp.int32, (rows, ck), 1)
                vis = kpos < qpos
            for hh in range(nh):
                z = _dot_nt(q_ref[hh, row0:, :], kc)
                lr = _log2_one_minus_sigmoid(z)
                if masked:
                    lr = jnp.where(vis, lr, 0.0)
                incl = _dot(jnp.concatenate(_split2(lr), axis=1), u)
                run = run_sc[hh, row0:, :]
                w = jnp.exp2(z + incl + jnp.concatenate([run] * (ck // LANES), axis=1))
                if masked:
                    w = jnp.where(vis, w, 0.0)
                acc_sc[hh, row0:, :] = acc_sc[hh, row0:, :] + _dot(w.astype(BF16), vc)
                run_sc[hh, row0:, :] = run + jnp.broadcast_to(incl[:, 0:1], (rows, LANES))

    @pl.when((flag & FLAG_MASK) != 0)
    def _():
        step(True)

    @pl.when((flag & FLAG_MASK) == 0)
    def _():
        step(False)

    @pl.when((flag & FLAG_LAST) != 0)
    def _():
        for hh in range(nh):
            o_ref[:, hh * HEAD_DIM:(hh + 1) * HEAD_DIM] = acc_sc[hh][:, :HEAD_DIM].astype(o_ref.dtype)


def _attn_sb(q_hm, k_hm, v_hm, batch, seq):
    nh = GROUP
    tq = tk = _row_tile(seq, 1024)
    nq = seq // tq
    ck = min(CUM_CHUNK, tk)
    qi, kj, fl = _schedule(nq, tq, descending=True, strict=True)
    u = (jnp.arange(ck)[:, None] >= jnp.arange(ck)[None, :]).astype(BF16)
    u = jnp.concatenate([u, u], axis=0)

    def qmap(b, g, s, qi, kj, fl):
        return (g, b * nq + qi[s], 0)

    def kmap(b, g, s, qi, kj, fl):
        return (g, b * nq + kj[s], 0)

    grid_spec = pltpu.PrefetchScalarGridSpec(
        num_scalar_prefetch=3, grid=(batch, N_KV, int(qi.shape[0])),
        in_specs=[pl.BlockSpec((nh, tq, HEAD_DIM), qmap), pl.BlockSpec((1, tk, HEAD_DIM), kmap),
                  pl.BlockSpec((1, tk, LANES), kmap), pl.BlockSpec((2 * ck, ck), lambda *_: (0, 0))],
        out_specs=pl.BlockSpec((tq, nh * HEAD_DIM), lambda b, g, s, qi, kj, fl: (b * nq + qi[s], g)),
        scratch_shapes=[pltpu.VMEM((nh, tq, LANES), F32), pltpu.VMEM((nh, tq, LANES), F32)])
    return pl.pallas_call(
        functools.partial(_attn_sb_body, nh=nh, tq=tq, tk=tk),
        grid_spec=grid_spec, out_shape=jax.ShapeDtypeStruct((batch * seq, Q_W), BF16),
        compiler_params=_cparams("arbitrary", "arbitrary", "arbitrary"), name="attn_sb",
    )(qi, kj, fl, q_hm, k_hm, v_hm, u)


def _suffix_sum_lanes(x):
    h, w = x.shape
    n = w // LANES
    lane = lax.broadcasted_iota(jnp.int32, (h, LANES), 1)
    slabs = []
    for i in range(n):
        y = x[:, i * LANES:(i + 1) * LANES]
        d = 1
        while d < LANES:
            y = y + jnp.where(lane < LANES - d, pltpu.roll(y, LANES - d, axis=1), 0.0)
            d *= 2
        slabs.append(y)
    tots = [jnp.broadcast_to(y[:, 0:1], (h, LANES)) for y in slabs]
    run = jnp.zeros((h, LANES), F32)
    out = [None] * n
    for i in reversed(range(n)):
        out[i] = slabs[i] + run
        run = run + tots[i]
    return (jnp.concatenate(out, axis=1) if n > 1 else out[0]), run


def _decode_body(pt_ref, *refs, mode, pb, hq, new_key):
    it = iter(refs)
    q_ref = next(it)
    qr_ref = next(it) if mode == "mla" else None
    n_pools = 3 if mode == "fox" else 2
    pools = [next(it) for _ in range(n_pools)]
    if new_key:
        kn_ref = next(it)
        krn_ref = next(it) if mode == "mla" else None
        vn_ref = kn_ref if mode == "mla" else next(it)
        lfn_ref = next(it) if mode == "fox" else None
    sink_ref = next(it) if mode == "swa" else None
    o_ref, m_sc, l_sc, acc_sc, car_sc = next(it), next(it), next(it), next(it), next(it)
    bufs = [next(it) for _ in range(n_pools)]
    sem = next(it)
    b = pl.program_id(0)
    t = pl.program_id(1)
    nt = pl.num_programs(1)
    n_pages = nt * pb
    step = b * nt + t
    slot = step % 2

    def page_copies(bb, tt, sl):
        out = []
        for r in range(pb):
            page = pt_ref[bb * n_pages + n_pages - 1 - (tt * pb + r)]
            for n in range(n_pools):
                out.append(pltpu.make_async_copy(pools[n].at[page], bufs[n].at[sl, r], sem.at[sl, n]))
        return out

    @pl.when(step == 0)
    def _():
        for c in page_copies(0, 0, 0):
            c.start()

    @pl.when(step + 1 < pl.num_programs(0) * nt)
    def _():
        last_t = t + 1 == nt
        for c in page_copies(jnp.where(last_t, b + 1, b), jnp.where(last_t, 0, t + 1), 1 - slot):
            c.start()

    for c in page_copies(b, t, slot):
        c.wait()

    k_refs = [bufs[0].at[slot, r] for r in range(pb)]
    kr_refs = [bufs[1].at[slot, r] for r in range(pb)] if mode == "mla" else None
    v_refs = k_refs if mode == "mla" else [bufs[1].at[slot, r] for r in range(pb)]
    lf_refs = [bufs[2].at[slot, r] for r in range(pb)] if mode == "fox" else None
    dk = q_ref.shape[-1]
    grouped = mode in ("sb", "fox", "swa")
    softmax = mode != "sb"

    if grouped:
        row_kv = lax.broadcasted_iota(jnp.int32, (hq, dk), 0) // GROUP
        lane_kv = lax.broadcasted_iota(jnp.int32, (hq, dk), 1) // HEAD_DIM
        q = jnp.where(row_kv == lane_kv, q_ref[0], 0.0)
    else:
        q = q_ref[0]
    qb = q.astype(BF16)

    @pl.when(t == 0)
    def _():
        if mode == "swa":
            m_sc[...] = jnp.broadcast_to(sink_ref[...] * LOG2E, m_sc.shape)
            l_sc[...] = jnp.ones_like(l_sc)
        else:
            m_sc[...] = jnp.full(m_sc.shape, NEG_BIG, F32)
            l_sc[...] = jnp.zeros_like(l_sc)
        acc_sc[...] = jnp.zeros_like(acc_sc)
        if mode == "fox" and new_key:
            car_sc[...] = jnp.broadcast_to(lfn_ref[0] * LOG2E, car_sc.shape)
        else:
            car_sc[...] = jnp.zeros_like(car_sc)

    order = list(reversed(range(pb)))
    s_parts = []
    for r in order:
        kp = k_refs[r][...].astype(BF16)
        if grouped:
            s = _dot(qb, kp)
        else:
            s = _dot_nt(qb, kp)
        if mode == "mla":
            s = s + _dot(qr_ref[0].astype(BF16), kr_refs[r][...].astype(BF16))
        s_parts.append(s)
    s = jnp.concatenate(s_parts, axis=1) if pb > 1 else s_parts[0]
    ps = s.shape[1] // pb
    nrep = s.shape[1] // LANES

    def wide(x):
        return jnp.concatenate([x] * nrep, axis=1) if nrep > 1 else x

    def pv(p):
        pb16 = p.astype(BF16)
        out = None
        for n, r in enumerate(order):
            blk = pb16[:, n * ps:(n + 1) * ps]
            vp = v_refs[r][...].astype(BF16)
            term = _dot_nt(blk, vp) if grouped else _dot(blk, vp)
            out = term if out is None else out + term
        return out

    if mode == "sb":
        lr = _log2_one_minus_sigmoid(s)
        incl, tot = _suffix_sum_lanes(lr)
        w = jnp.exp2(s + incl + wide(car_sc[...]))
        acc_sc[...] = acc_sc[...] + pv(w)
        car_sc[...] = car_sc[...] + tot
    else:
        if mode == "fox":
            lf = jnp.concatenate([lf_refs[r][...] for r in order], axis=1) if pb > 1 else lf_refs[0][...]
            lf = lf * LOG2E
            incl, tot = _suffix_sum_lanes(lf)
            s = s + (incl - lf) + wide(car_sc[...])
            car_sc[...] = car_sc[...] + tot
        m_prev = m_sc[...]
        m_new = jnp.maximum(m_prev, jnp.max(s, axis=-1, keepdims=True))
        alpha = jnp.exp2(m_prev - m_new)
        p = jnp.exp2(s - wide(m_new))
        l_sc[...] = alpha * l_sc[...] + jnp.sum(p, axis=-1, keepdims=True)
        nacc = acc_sc.shape[-1] // LANES
        alpha_w = jnp.concatenate([alpha] * nacc, axis=1) if nacc > 1 else alpha
        acc_sc[...] = alpha_w * acc_sc[...] + pv(p)
        m_sc[...] = m_new

    @pl.when(t == pl.num_programs(1) - 1)
    def _():
        acc = acc_sc[...]
        if softmax:
            m_prev, l_prev = m_sc[...][:, :1], l_sc[...][:, :1]
            if new_key:
                s_new = jnp.sum(q * kn_ref[0], axis=-1, keepdims=True)
                if mode == "mla":
                    s_new = s_new + jnp.sum(qr_ref[0] * krn_ref[0], axis=-1, keepdims=True)
                m_new = jnp.maximum(m_prev, s_new)
                alpha = jnp.exp2(m_prev - m_new)
                p_new = jnp.exp2(s_new - m_new)
                l_prev = alpha * l_prev + p_new
                acc = alpha * acc + p_new * vn_ref[0]
            acc = acc / l_prev
        if grouped:
            row_kv = lax.broadcasted_iota(jnp.int32, (hq, HEAD_DIM), 0) // GROUP
            out = jnp.zeros((hq, HEAD_DIM), F32)
            for kv in range(N_KV):
                out = out + jnp.where(row_kv == kv, acc[:, kv * HEAD_DIM:(kv + 1) * HEAD_DIM], 0.0)
            o_ref[0] = out
        else:
            o_ref[0] = acc


def _decode(mode, q, k_pool, v_pool, page_table, *, pb, qr=None, kr_pool=None,
            lf_pool=None, k_new=None, v_new=None, kr_new=None, lf_new=None, sinks=None):
    bsz, hq, dk = q.shape
    n_pages = page_table.shape[1]
    assert n_pages % pb == 0
    grouped = mode in ("sb", "fox", "swa")
    vblk = (v_pool if v_pool is not None else k_pool).shape[1:]
    dvt = vblk[0] if grouped else vblk[1]
    out_d = HEAD_DIM if grouped else dvt
    new_key = k_new is not None
    pt = page_table.reshape(-1).astype(jnp.int32)

    def seq_map(b, t, pt):
        return (b, 0, 0)

    in_specs = [pl.BlockSpec((1, hq, dk), seq_map)]
    args = [q]
    if mode == "mla":
        in_specs.append(pl.BlockSpec((1, hq, qr.shape[-1]), seq_map))
        args.append(qr)
    pools = [k_pool, kr_pool if mode == "mla" else v_pool] + ([lf_pool] if mode == "fox" else [])
    for p in pools:
        in_specs.append(pl.BlockSpec(memory_space=pl.ANY))
        args.append(p)
    if new_key:
        in_specs.append(pl.BlockSpec((1, 1, dk), seq_map))
        args.append(k_new)
        if mode == "mla":
            in_specs.append(pl.BlockSpec((1, 1, kr_new.shape[-1]), seq_map))
            args.append(kr_new)
        else:
            in_specs.append(pl.BlockSpec((1, 1, dvt), seq_map))
            args.append(v_new)
        if mode == "fox":
            in_specs.append(pl.BlockSpec((1, hq, 1), seq_map))
            args.append(lf_new)
    if mode == "swa":
        in_specs.append(pl.BlockSpec((hq, 1), lambda b, t, pt: (0, 0)))
        args.append(sinks.reshape(hq, 1).astype(F32))
    grid_spec = pltpu.PrefetchScalarGridSpec(
        num_scalar_prefetch=1, grid=(bsz, n_pages // pb), in_specs=in_specs,
        out_specs=pl.BlockSpec((1, hq, out_d), seq_map),
        scratch_shapes=[pltpu.VMEM((hq, LANES), F32), pltpu.VMEM((hq, LANES), F32),
                        pltpu.VMEM((hq, dvt), F32), pltpu.VMEM((hq, LANES), F32)]
        + [pltpu.VMEM((2, pb) + p.shape[1:], p.dtype) for p in pools]
        + [pltpu.SemaphoreType.DMA((2, len(pools)))])
    return pl.pallas_call(
        functools.partial(_decode_body, mode=mode, pb=pb, hq=hq, new_key=new_key),
        grid_spec=grid_spec, out_shape=jax.ShapeDtypeStruct((bsz, hq, out_d), F32),
        compiler_params=_cparams("arbitrary", "arbitrary"), name="decode_" + mode,
    )(pt, *args)


def _cross_decode_body(q_ref, k_ref, v_ref, o_ref, *, nseq):
    rows = k_ref.shape[1]
    col_head = lax.broadcasted_iota(jnp.int32, (SUBLANES, rows), 1) % MEM_HEADS
    own = col_head == lax.broadcasted_iota(jnp.int32, (SUBLANES, rows), 0)
    for n in range(nseq):
        s = _dot_nt(q_ref[n].astype(BF16), k_ref[n].astype(BF16))
        s = jnp.where(own, s, NEG_BIG)
        p = jnp.exp2(s - jnp.max(s, axis=-1, keepdims=True))
        o = _dot(p.astype(BF16), v_ref[n].astype(BF16))
        o_ref[n] = o / jnp.sum(p, axis=-1, keepdims=True)


def _cross_decode(q, k_rows, v_rows, layer):
    bsz = q.shape[0]
    nseq = math.gcd(bsz, 8)
    steps = bsz // nseq
    blk = (nseq,) + k_rows.shape[1:]
    return pl.pallas_call(
        functools.partial(_cross_decode_body, nseq=nseq), grid=(steps,),
        in_specs=[pl.BlockSpec((nseq, SUBLANES, MEM_HEAD_DIM), lambda i: (i, 0, 0)),
                  pl.BlockSpec(blk, lambda i: (layer * steps + i, 0, 0)),
                  pl.BlockSpec(blk, lambda i: (layer * steps + i, 0, 0))],
        out_specs=pl.BlockSpec((nseq, SUBLANES, MEM_HEAD_DIM), lambda i: (i, 0, 0)),
        out_shape=jax.ShapeDtypeStruct((bsz, SUBLANES, MEM_HEAD_DIM), F32),
        compiler_params=_cparams("arbitrary"), name="decode_cross",
    )(q, k_rows, v_rows)


def _heads_matmul_body(x_ref, w_ref, o_ref):
    o_ref[0] = _dot(x_ref[0].astype(BF16), w_ref[0])


def _heads_matmul(x, w):
    h, m, k = x.shape
    n = w.shape[-1]
    return pl.pallas_call(
        _heads_matmul_body, grid=(h,),
        in_specs=[pl.BlockSpec((1, m, k), lambda i: (i, 0, 0)), pl.BlockSpec((1, k, n), lambda i: (i, 0, 0))],
        out_specs=pl.BlockSpec((1, m, n), lambda i: (i, 0, 0)),
        out_shape=jax.ShapeDtypeStruct((h, m, n), F32),
        compiler_params=_cparams("arbitrary"), name="mla_uv",
    )(x, w)


def _oproj_body(x_ref, o_ref, wo_ref, gc_ref, wcq_ref, *refs, cross):
    if cross:
        mk_ref, mv_ref, x1_ref, oc_ref = refs
    else:
        x1_ref, qc_ref = refs
    x1 = x_ref[...] + _dot(o_ref[...].astype(BF16), wo_ref[...])
    x1_ref[...] = x1
    qc = _dot(_rms(x1, gc_ref[...]).astype(BF16), wcq_ref[...]) * (MEM_HEAD_DIM ** -0.5 * LOG2E)
    if not cross:
        qc_ref[...] = qc
        return
    qc = qc.astype(BF16)
    for hh in range(MEM_HEADS):
        sl = slice(hh * MEM_HEAD_DIM, (hh + 1) * MEM_HEAD_DIM)
        s = _dot_nt(qc[:, sl], mk_ref[:, sl])
        p = jnp.exp2(s - jnp.max(s, axis=-1, keepdims=True))
        o = _dot(p.astype(BF16), mv_ref[:, sl]) / jnp.sum(p, axis=-1, keepdims=True)
        oc_ref[:, sl] = o.astype(oc_ref.dtype)


def _oproj(x, o, w_o, g_c, w_cq, *, mem=None, seq_len=1):
    m = x.shape[0]
    tm = _row_tile(min(m, seq_len) if mem is not None else m, 512)
    cross = mem is not None
    in_specs = [pl.BlockSpec((tm, D_MODEL), lambda i: (i, 0)), pl.BlockSpec((tm, o.shape[1]), lambda i: (i, 0)),
                _full(w_o.shape), _full((1, D_MODEL)), _full(w_cq.shape)]
    args = [x, o, w_o, g_c.reshape(1, D_MODEL), w_cq]
    if cross:
        n_mem = mem[0].shape[0] // (m // seq_len)
        tiles_per_seq = seq_len // tm
        for a in mem:
            in_specs.append(pl.BlockSpec((n_mem, MEM_W), lambda i: (i // tiles_per_seq, 0)))
            args.append(a)
    out_shape = [jax.ShapeDtypeStruct((m, D_MODEL), F32),
                 jax.ShapeDtypeStruct((m, MEM_W), BF16 if cross else F32)]
    out_specs = [pl.BlockSpec((tm, D_MODEL), lambda i: (i, 0)), pl.BlockSpec((tm, MEM_W), lambda i: (i, 0))]
    return pl.pallas_call(
        functools.partial(_oproj_body, cross=cross), grid=(m // tm,),
        in_specs=in_specs, out_specs=out_specs, out_shape=out_shape,
        compiler_params=_cparams("arbitrary"), name="oproj_cross" if cross else "oproj",
    )(*args)


def _ffn_body(*refs, seq_mode, final, nseq_tiles, tm, fc):
    it = iter(refs)
    x1_ref, oc_ref, wco_ref, gf_ref = next(it), next(it), next(it), next(it)
    wg_ref, wv_ref, wc_ref, bc_ref, wout_ref = next(it), next(it), next(it), next(it), next(it)
    if not seq_mode:
        b0_ref, b1_ref = next(it), next(it)
    if final:
        gfin_ref = next(it)
    x3_ref, gate_ref = next(it), next(it)
    if final:
        y_ref = next(it)
    x2_sc, h_sc, acc_sc = next(it), next(it), next(it)
    if seq_mode:
        halo_sc, gs_sc = next(it), next(it)
    i = pl.program_id(0)
    f = pl.program_id(1)

    @pl.when(f == 0)
    def _():
        x2 = x1_ref[...] + _dot(oc_ref[...].astype(BF16), wco_ref[...])
        x2_sc[...] = x2
        h_sc[...] = _rms(x2, gf_ref[...]).astype(BF16)
        acc_sc[...] = jnp.zeros_like(acc_sc)

    h = h_sc[...]
    gate = _dot(h, wg_ref[...])
    val = _dot(h, wv_ref[...])
    wc = wc_ref[...]
    if seq_mode:
        @pl.when(i % nseq_tiles == 0)
        def _():
            gs_sc[0:SUBLANES, :] = jnp.zeros((SUBLANES, fc), F32)

        @pl.when(i % nseq_tiles != 0)
        def _():
            gs_sc[0:SUBLANES, :] = halo_sc[f]

        gs_sc[SUBLANES:, :] = gate
        gm1 = gs_sc[SUBLANES - 1:SUBLANES - 1 + tm, :]
        gm2 = gs_sc[SUBLANES - 2:SUBLANES - 2 + tm, :]
        tail = gate[tm - SUBLANES:, :]
        halo_sc[f] = tail
        gate_ref[0] = tail
    else:
        gm2, gm1 = b0_ref[...], b1_ref[...]
        gate_ref[...] = gate
    conv = bc_ref[...] + wc[0:1, :] * gm2 + wc[1:2, :] * gm1 + wc[2:3, :] * gate
    y = 0.5 * conv * (1.0 + lax.erf(conv * (2.0 ** -0.5))) * val
    acc_sc[...] = acc_sc[...] + _dot(y.astype(BF16), wout_ref[...])

    @pl.when(f == pl.num_programs(1) - 1)
    def _():
        x3 = x2_sc[...] + acc_sc[...]
        x3_ref[...] = x3
        if final:
            y_ref[...] = _rms(x3, gfin_ref[...])


def _ffn(x1, oc, w_co, g_f, w_in, w_conv, b_conv, w_out, *, seq_len=1, buf=None, g_final=None):
    m = x1.shape[0]
    seq_mode = buf is None
    final = g_final is not None
    tm = _row_tile(min(m, seq_len) if seq_mode else m, 512)
    nseq_tiles = max(seq_len // tm, 1)
    fc = D_FF
    nf = D_FF // fc
    row = lambda i, f: (i, 0)
    in_specs = [pl.BlockSpec((tm, D_MODEL), row), pl.BlockSpec((tm, MEM_W), row), _full(w_co.shape),
                _full((1, D_MODEL)),
                pl.BlockSpec((D_MODEL, fc), lambda i, f: (0, f)),
                pl.BlockSpec((D_MODEL, fc), lambda i, f: (0, nf + f)),
                pl.BlockSpec((CONV_W, fc), lambda i, f: (0, f)),
                pl.BlockSpec((1, fc), lambda i, f: (0, f)),
                pl.BlockSpec((fc, D_MODEL), lambda i, f: (f, 0))]
    args = [x1, oc, w_co, g_f.reshape(1, D_MODEL), w_in, w_in, w_conv, b_conv.reshape(1, D_FF), w_out]
    if not seq_mode:
        in_specs += [pl.BlockSpec((tm, fc), lambda i, f: (i, f))] * 2
        args += [buf[0], buf[1]]
    if final:
        in_specs.append(_full((1, D_MODEL)))
        args.append(g_final.reshape(1, D_MODEL))
    out_shape = [jax.ShapeDtypeStruct((m, D_MODEL), F32)]
    out_specs = [pl.BlockSpec((tm, D_MODEL), row)]
    if seq_mode:
        out_shape.append(jax.ShapeDtypeStruct((m // tm, SUBLANES, D_FF), F32))
        out_specs.append(pl.BlockSpec((1, SUBLANES, fc), lambda i, f: (i, 0, f)))
    else:
        out_shape.append(jax.ShapeDtypeStruct((m, D_FF), F32))
        out_specs.append(pl.BlockSpec((tm, fc), lambda i, f: (i, f)))
    if final:
        out_shape.append(jax.ShapeDtypeStruct((m, D_MODEL), F32))
        out_specs.append(pl.BlockSpec((tm, D_MODEL), row))
    scratch = [pltpu.VMEM((tm, D_MODEL), F32), pltpu.VMEM((tm, D_MODEL), BF16), pltpu.VMEM((tm, D_MODEL), F32)]
    if seq_mode:
        scratch += [pltpu.VMEM((nf, SUBLANES, fc), F32), pltpu.VMEM((tm + SUBLANES, fc), F32)]
    return pl.pallas_call(
        functools.partial(_ffn_body, seq_mode=seq_mode, final=final, nseq_tiles=nseq_tiles, tm=tm, fc=fc),
        grid=(m // tm, nf), in_specs=in_specs, out_specs=out_specs, out_shape=out_shape,
        scratch_shapes=scratch, compiler_params=_cparams("arbitrary", "arbitrary"),
        name="ffn_seq" if seq_mode else "ffn_tok",
    )(*args)


def _rope_tables(pos, half, period, lane_off):
    inv = jnp.exp(-math.log(ROPE_THETA) * jnp.arange(half, dtype=F32) / half)
    ang = pos.astype(F32)[:, None] * inv[None, :]
    cos, sin = jnp.cos(ang), jnp.sin(ang)
    n = pos.shape[0]
    one = jnp.ones((n, lane_off), F32)
    zero = jnp.zeros((n, lane_off), F32)
    padw = period - lane_off - 2 * half
    pad1, pad0 = jnp.ones((n, padw), F32), jnp.zeros((n, padw), F32)
    z = jnp.zeros_like(sin)
    cos_t = jnp.concatenate([one, cos, cos, pad1], axis=1)
    sin_a = jnp.concatenate([zero, -sin, z, pad0], axis=1)
    sin_b = jnp.concatenate([zero, z, sin, pad0], axis=1)
    rep = LANES // period
    return tuple(jnp.tile(t, (1, rep)) for t in (cos_t, sin_a, sin_b))


def _mla_weights(w_in, g_q, g_kv, w_uq, w_uk, w_uv):
    kr_cols = w_in[:, MLA_Q_RANK + MLA_KV_RANK:]
    slab = jnp.concatenate([jnp.zeros((D_MODEL, MLA_NOPE), F32), kr_cols,
                            jnp.zeros((D_MODEL, LANES - MLA_QK), F32)], axis=1)
    w_in_p = jnp.concatenate([w_in[:, :MLA_Q_RANK + MLA_KV_RANK], slab], axis=1)
    uq = w_uq.reshape(MLA_Q_RANK, N_HEADS, MLA_QK)
    uq = jnp.pad(uq, ((0, 0), (0, 0), (0, LANES - MLA_QK))).reshape(MLA_Q_RANK, N_HEADS * LANES)
    uk = jnp.pad(w_uk, ((0, 0), (0, 0), (0, LANES - MLA_NOPE))).reshape(MLA_KV_RANK, N_HEADS * LANES)
    return dict(
        w_in=w_in_p.astype(BF16), g_q=g_q.reshape(1, MLA_Q_RANK), g_kv=g_kv.reshape(1, MLA_KV_RANK),
        w_uq=uq.astype(BF16), w_uk=uk.astype(BF16),
        w_uv=w_uv.reshape(MLA_KV_RANK, N_HEADS * MLA_V).astype(BF16),
        w_ukt=jnp.transpose(w_uk, (1, 2, 0)).astype(BF16),
        w_uvh=jnp.transpose(w_uv, (1, 0, 2)).astype(BF16),
    )


def _repeat_q(q_hm, n_rep):
    q = jnp.transpose(q_hm, (1, 0, 2)).astype(F32)
    return jnp.tile(q, (1, 1, n_rep))


def _kv_pages_t(cache):
    p, s = cache.shape[0], cache.shape[1]
    return jnp.transpose(cache, (0, 2, 3, 1)).reshape(p, KV_W, s)


def kernel(x_prompt, x_sample, cache_sb_k, cache_sb_v, cache_mla_ckv, cache_mla_kr, cache_fox_k, cache_fox_v, cache_fox_logf, state_swa_k, state_swa_v, cache_mem_k, cache_mem_v, state_ffn_conv, page_table, mem_prompt, g_mix, w_sb_qkv, w_sb_o, w_mla_in, g_mla_q, g_mla_kv, w_mla_uq, w_mla_uk, w_mla_uv, w_mla_o, w_fox_in, b_fox_f, w_fox_o, w_swa_qkv, swa_sinks, w_swa_o, g_cross, g_mem, w_cq, w_ckv, w_co, g_ffn, w_ffn_in, w_ffn_conv, b_ffn_conv, w_ffn_out, g_final):
    bp, seq, _ = x_prompt.shape
    bs = x_sample.shape[0]
    n_pages = page_table.shape[1]
    past = n_pages * PAGE_SIZE
    depth = g_mix.shape[0]
    n_mem = mem_prompt.shape[1]
    mp = bp * seq
    win = state_swa_k.shape[2]
    pb = math.gcd(n_pages, 32)

    bf = lambda a: a.astype(BF16)
    pos_p = jnp.arange(seq)
    pos_s = jnp.full((bs,), past, jnp.int32)
    swa_tabs_p = _rope_tables(pos_p, HEAD_DIM // 2, HEAD_DIM, 0)
    swa_tabs_s = _rope_tables(pos_s, HEAD_DIM // 2, HEAD_DIM, 0)
    mla_tabs_p = _rope_tables(pos_p, MLA_ROPE // 2, LANES, MLA_NOPE)
    mla_tabs_s = _rope_tables(pos_s, MLA_ROPE // 2, LANES, MLA_NOPE)
    ident_pages = jnp.arange(bs, dtype=jnp.int32).reshape(bs, 1)
    mem_k_rows = cache_mem_k.reshape(depth * bs, n_mem * MEM_HEADS, MEM_HEAD_DIM)
    mem_v_rows = cache_mem_v.reshape(depth * bs, n_mem * MEM_HEADS, MEM_HEAD_DIM)

    outs = {}
    xp = x_prompt.reshape(mp, D_MODEL)
    xs = x_sample.reshape(bs, D_MODEL)
    mem_flat = mem_prompt.reshape(bp * n_mem, D_MODEL)
    mem_k_p, mem_v_p, conv_p, conv_s = [], [], [], []
    y_p = y_s = None

    for i in range(depth):
        m, j = i % 4, i // 4
        gm = g_mix[i]
        if m == 0:
            w = bf(w_sb_qkv[j])
            q, k, v, kh, vh = _proj_qkv(xp, gm, w)
            o_p = _attn_sb(q, kh, vh, bp, seq)
            outs.setdefault("sb_k_p", []).append(k.reshape(bp, seq, N_KV, HEAD_DIM))
            outs.setdefault("sb_v_p", []).append(v.reshape(bp, seq, N_KV, HEAD_DIM))
            q, k, v, kh, vh = _proj_qkv(xs, gm, w)
            o_s = _decode("sb", _repeat_q(q, N_KV), _kv_pages_t(cache_sb_k[j]), _kv_pages_t(cache_sb_v[j]),
                          page_table, pb=pb)
            o_s = o_s.reshape(bs, Q_W)
            outs.setdefault("sb_k_s", []).append(k.reshape(bs, 1, N_KV, HEAD_DIM))
            outs.setdefault("sb_v_s", []).append(v.reshape(bs, 1, N_KV, HEAD_DIM))
            w_o = bf(w_sb_o[j])
        elif m == 1:
            wp = _mla_weights(w_mla_in[j], g_mla_q[j], g_mla_kv[j], w_mla_uq[j], w_mla_uk[j], w_mla_uv[j])
            q, kh, vh, ckv, kr = _proj_mla(xp, gm, wp, mla_tabs_p, absorbed=False)
            o_p = _attn("mla", q, kh, vh, bp, seq)
            outs.setdefault("mla_ckv_p", []).append(ckv.reshape(bp, seq, MLA_KV_RANK))
            outs.setdefault("mla_kr_p", []).append(kr[:, MLA_NOPE:MLA_QK].reshape(bp, seq, MLA_ROPE))
            q, kh, vh, ckv, kr, qlat = _proj_mla(xs, gm, wp, mla_tabs_s, absorbed=True)
            kr32 = kr[:, MLA_NOPE:MLA_QK]
            qr = jnp.transpose(q[:, :, MLA_NOPE:MLA_QK], (1, 0, 2)).astype(F32)
            o_lat = _decode("mla", jnp.transpose(qlat, (1, 0, 2)), cache_mla_ckv[j], None, page_table,
                            pb=pb, qr=qr, kr_pool=jnp.transpose(cache_mla_kr[j], (0, 2, 1)),
                            k_new=ckv.reshape(bs, 1, MLA_KV_RANK), kr_new=kr32.reshape(bs, 1, MLA_ROPE))
            o_s = _heads_matmul(jnp.transpose(o_lat, (1, 0, 2)), wp["w_uvh"])
            o_s = jnp.transpose(o_s, (1, 0, 2)).reshape(bs, Q_W)
            outs.setdefault("mla_ckv_s", []).append(ckv.reshape(bs, 1, MLA_KV_RANK))
            outs.setdefault("mla_kr_s", []).append(kr32.reshape(bs, 1, MLA_ROPE))
            w_o = bf(w_mla_o[j])
        elif m == 2:
            w = bf(w_fox_in[j])
            q, k, v, kh, vh, lf, c = _proj_qkv(xp, gm, w, b_f=b_fox_f[j], seq_len=seq)
            cq = jnp.transpose(c.reshape(mp, N_KV, GROUP), (1, 0, 2))
            ck = jnp.transpose(c.reshape(mp, N_KV, GROUP), (1, 2, 0))
            o_p = _attn("fox", q, kh, vh, bp, seq, cq=cq, ck=ck)
            outs.setdefault("fox_k_p", []).append(k.reshape(bp, seq, N_KV, HEAD_DIM))
            outs.setdefault("fox_v_p", []).append(v.reshape(bp, seq, N_KV, HEAD_DIM))
            outs.setdefault("fox_lf_p", []).append(lf.reshape(bp, seq, N_HEADS))
            q, k, v, kh, vh, lf, c = _proj_qkv(xs, gm, w, b_f=b_fox_f[j], seq_len=1)
            o_s = _decode("fox", _repeat_q(q, N_KV), _kv_pages_t(cache_fox_k[j]), _kv_pages_t(cache_fox_v[j]),
                          page_table, pb=pb, lf_pool=jnp.transpose(cache_fox_logf[j], (0, 2, 1)),
                          k_new=k.reshape(bs, 1, KV_W), v_new=v.reshape(bs, 1, KV_W),
                          lf_new=lf.reshape(bs, N_HEADS, 1))
            o_s = o_s.reshape(bs, Q_W)
            outs.setdefault("fox_k_s", []).append(k.reshape(bs, 1, N_KV, HEAD_DIM))
            outs.setdefault("fox_v_s", []).append(v.reshape(bs, 1, N_KV, HEAD_DIM))
            outs.setdefault("fox_lf_s", []).append(lf.reshape(bs, 1, N_HEADS))
            w_o = bf(w_fox_o[j])
        else:
            w = bf(w_swa_qkv[j])
            q, k, v, kh, vh = _proj_qkv(xp, gm, w, rope_tabs=swa_tabs_p)
            o_p = _attn("swa", q, kh, vh, bp, seq, sinks=swa_sinks[j])
            outs.setdefault("swa_k_p", []).append(k.reshape(bp, seq, N_KV, HEAD_DIM)[:, seq - win:])
            outs.setdefault("swa_v_p", []).append(v.reshape(bp, seq, N_KV, HEAD_DIM)[:, seq - win:])
            q, k, v, kh, vh = _proj_qkv(xs, gm, w, rope_tabs=swa_tabs_s)
            o_s = _decode("swa", _repeat_q(q, N_KV), _kv_pages_t(state_swa_k[j]), _kv_pages_t(state_swa_v[j]),
                          ident_pages, pb=1, k_new=k.reshape(bs, 1, KV_W), v_new=v.reshape(bs, 1, KV_W),
                          sinks=swa_sinks[j])
            o_s = o_s.reshape(bs, Q_W)
            k_all = jnp.concatenate([state_swa_k[j], k.reshape(bs, 1, N_KV, HEAD_DIM)], axis=1)
            v_all = jnp.concatenate([state_swa_v[j], v.reshape(bs, 1, N_KV, HEAD_DIM)], axis=1)
            outs.setdefault("swa_k_s", []).append(k_all[:, -win:])
            outs.setdefault("swa_v_s", []).append(v_all[:, -win:])
            w_o = bf(w_swa_o[j])

        kv_f32, kv_b16 = _proj_plain(mem_flat, g_mem[i], bf(w_ckv[i]))
        mem_k_p.append(kv_f32[:, :MEM_W].reshape(bp, n_mem, MEM_HEADS, MEM_HEAD_DIM))
        mem_v_p.append(kv_f32[:, MEM_W:].reshape(bp, n_mem, MEM_HEADS, MEM_HEAD_DIM))
        w_cq_b, w_co_b = bf(w_cq[i]), bf(w_co[i])
        x1_p, oc_p = _oproj(xp, o_p, w_o, g_cross[i], w_cq_b,
                            mem=(kv_b16[:, :MEM_W], kv_b16[:, MEM_W:]), seq_len=seq)
        x1_s, qc_s = _oproj(xs, o_s, w_o, g_cross[i], w_cq_b)
        qc_rows = jnp.pad(qc_s.reshape(bs, MEM_HEADS, MEM_HEAD_DIM), ((0, 0), (0, SUBLANES - MEM_HEADS), (0, 0)))
        oc_s = _cross_decode(qc_rows, mem_k_rows, mem_v_rows, i)
        oc_s = oc_s[:, :MEM_HEADS].reshape(bs, MEM_W)

        fin = g_final if i == depth - 1 else None
        w_in_b, w_out_b = bf(w_ffn_in[i]), bf(w_ffn_out[i])
        res = _ffn(x1_p, oc_p, w_co_b, g_ffn[i], w_in_b, w_ffn_conv[i], b_ffn_conv[i], w_out_b,
                   seq_len=seq, g_final=fin)
        xp, tails = res[0], res[1]
        if fin is not None:
            y_p = res[2]
        tps = tails.shape[0] // bp
        conv_p.append(tails.reshape(bp, tps, SUBLANES, D_FF)[:, -1, SUBLANES - (CONV_W - 1):])
        buf = state_ffn_conv[i]
        res = _ffn(x1_s, oc_s, w_co_b, g_ffn[i], w_in_b, w_ffn_conv[i], b_ffn_conv[i], w_out_b,
                   buf=(buf[:, 0], buf[:, 1]), g_final=fin)
        xs, gate_s = res[0], res[1]
        if fin is not None:
            y_s = res[2]
        conv_s.append(jnp.stack([buf[:, 1], gate_s], axis=1))

    stk = lambda name: jnp.stack(outs[name])
    return (y_p.reshape(bp, seq, D_MODEL), y_s.reshape(bs, 1, D_MODEL),
            stk("sb_k_p"), stk("sb_v_p"), stk("sb_k_s"), stk("sb_v_s"),
            stk("mla_ckv_p"), stk("mla_kr_p"), stk("mla_ckv_s"), stk("mla_kr_s"),
            stk("fox_k_p"), stk("fox_v_p"), stk("fox_lf_p"),
            stk("fox_k_s"), stk("fox_v_s"), stk("fox_lf_s"),
            stk("swa_k_p"), stk("swa_v_p"), stk("swa_k_s"), stk("swa_v_s"),
            jnp.stack(mem_k_p), jnp.stack(mem_v_p),
            jnp.stack(conv_p), jnp.stack(conv_s))
```
